```python
import jax, jax.numpy as jnp
from jax import lax
import numpy as np

D_MODEL = 2048
BATCH = 4
SEQ = 2048
DEPTH = 2
DEC_BATCH = 128
DEC_SEQ = 8
PAST_LEN = 16384
PAGE_SIZE = 128

RET_HEADS = 8
RET_DK = 64
RET_DV = 128
RET_QK = RET_HEADS * RET_DK
RET_V = RET_HEADS * RET_DV
RET_CHUNK = 128
ROPE_BASE = 10000.0
RWKV_HEAD = 64
RWKV_HEADS = 16
RWKV_W = RWKV_HEADS * RWKV_HEAD
DECAY_LORA = 64
ICLR_LORA = 64
GN_EPS_RWKV = 64e-5
LRU_W = 2048
LRU_BLOCKS = 8
LRU_BW = LRU_W // LRU_BLOCKS
CONV_W = 4
LRU_C = 8.0
NORM_EPS = 1e-6
RKV_START = 2 * RET_QK + 2 * RET_V
RKV_END = RKV_START + 3 * RWKV_W
IN0 = RKV_END + RWKV_W
SPLIT0 = (RET_QK, 2 * RET_QK, 2 * RET_QK + RET_V, RKV_START, RKV_END)
MIX0 = RET_V + RWKV_W

kernel_name = 'hybrid_retention_rwkv7_rglru_step'


def _rmsnorm(x, g):
    xf = x.astype(jnp.float32)
    return xf * lax.rsqrt(jnp.mean(xf * xf, axis=-1, keepdims=True) + NORM_EPS) * g.astype(jnp.float32)


def _rotary(x, pos):
    d = x.shape[-1]
    inv = 1.0 / (ROPE_BASE ** (jnp.arange(0, d, 2, dtype=jnp.float32) / d))
    ang = pos.astype(jnp.float32)[:, None] * inv[None, :]
    cos = jnp.cos(ang)[None, :, None, :]
    sin = jnp.sin(ang)[None, :, None, :]
    x1, x2 = x[..., : d // 2], x[..., d // 2:]
    return jnp.concatenate([x1 * cos - x2 * sin, x1 * sin + x2 * cos], axis=-1)


def _retention_chunk(S, q, k, v, log_gamma):
    C = q.shape[2]
    n = jnp.arange(C, dtype=jnp.float32)
    diff = n[:, None] - n[None, :]
    dmask = jnp.where(diff >= 0, jnp.exp(log_gamma[:, None, None] * jnp.maximum(diff, 0.0)), 0.0)
    scores = jnp.einsum('bhnd,bhmd->bhnm', q, k) * dmask
    o = jnp.einsum('bhnm,bhmv->bhnv', scores, v)
    o = o + jnp.einsum('bhnd,bhdv->bhnv', q, S) * jnp.exp(log_gamma[:, None] * (n + 1.0))[:, :, None]
    k_dec = k * jnp.exp(log_gamma[:, None] * (C - 1.0 - n))[:, :, None]
    S = S * jnp.exp(log_gamma * C)[:, None, None] + jnp.einsum('bhmd,bhmv->bhdv', k_dec, v)
    return S, o


def _retention(q, k, v, S0):
    B, H, T, _ = q.shape
    log_gamma = jnp.log(1.0 - 2.0 ** (-5.0 - jnp.arange(RET_HEADS, dtype=jnp.float32)))
    C = RET_CHUNK if T % RET_CHUNK == 0 else T
    nc = T // C

    def split(a):
        return a.reshape(B, H, nc, C, a.shape[-1]).transpose(2, 0, 1, 3, 4)

    def step(S, blk):
        qc, kc, vc = blk
        return _retention_chunk(S, qc, kc, vc, log_gamma)

    S_T, o = lax.scan(step, S0, (split(q), split(k), split(v)))
    o = o.transpose(1, 2, 0, 3, 4).reshape(B, H, T, -1)
    return o, S_T


def _rwkv7_scan(S0, r, w, k, v, a_vec, b_vec):
    def step(S, inp):
        r_t, w_t, k_t, v_t, a_t, b_t = inp
        sa = jnp.einsum('bhij,bhj->bhi', S, a_t)
        S = S * w_t[:, :, None, :] + sa[..., None] * b_t[:, :, None, :] + v_t[..., None] * k_t[:, :, None, :]
        y = jnp.einsum('bhij,bhj->bhi', S, r_t)
        return S, y

    xs = tuple(t.transpose(1, 0, 2, 3) for t in (r, w, k, v, a_vec, b_vec))
    S_T, y = lax.scan(step, S0, xs)
    return y.transpose(1, 0, 2, 3), S_T


def _layer_ret_rwkv(x, pos, s_ret, s_wkv, x_prev, norm_g, w_in, w_out, mu_rkv, mu_w, mu_a,
                    w0, w1, w2, a0, a1, a2, k_k, k_a, r_k, gn_g, gn_b):
    f32 = jnp.float32
    B, T, _ = x.shape
    h = _rmsnorm(x, norm_g)
    p = jnp.einsum('btd,de->bte', h, w_in).astype(f32)
    q, k, v, g_ret, p_rkv, g_rwkv = jnp.split(p, SPLIT0, axis=-1)

    q = _rotary(q.reshape(B, T, RET_HEADS, RET_DK), pos)
    k = _rotary(k.reshape(B, T, RET_HEADS, RET_DK), pos) * (RET_DK ** -0.5)
    v = v.reshape(B, T, RET_HEADS, RET_DV)
    o, s_ret_new = _retention(q.transpose(0, 2, 1, 3), k.transpose(0, 2, 1, 3),
                              v.transpose(0, 2, 1, 3), s_ret.astype(f32))
    o = o * lax.rsqrt(jnp.mean(o * o, axis=-1, keepdims=True) + NORM_EPS)
    ret_out = o.transpose(0, 2, 1, 3).reshape(B, T, RET_V) * jax.nn.silu(g_ret)

    x_prev = x_prev.astype(f32)
    h_shift = jnp.concatenate([x_prev[:, None], h[:, :-1]], axis=1)
    p_first = jnp.einsum('bd,de->be', x_prev, w_in[:, RKV_START:RKV_END]).astype(f32)
    p_shift = jnp.concatenate([p_first[:, None], p_rkv[:, :-1]], axis=1)
    p_mix = p_rkv + mu_rkv * (p_shift - p_rkv)
    r, kw, vw = jnp.split(p_mix, 3, axis=-1)
    xx = h_shift - h
    xw = h + mu_w * xx
    xa = h + mu_a * xx
    w_log = -jax.nn.softplus(-(w0 + jnp.tanh(xw @ w1) @ w2)) - 0.5
    decay = jnp.exp(-jnp.exp(w_log))
    a = jax.nn.sigmoid(a0 + (xa @ a1) @ a2)

    def hd(t):
        return t.reshape(B, T, RWKV_HEADS, RWKV_HEAD)

    kk = hd(kw * k_k)
    kk = kk / jnp.maximum(jnp.sqrt(jnp.sum(kk * kk, axis=-1, keepdims=True)), 1e-12)
    kw = kw * (1.0 + (a - 1.0) * k_a)
    r_h, k_h, v_h, a_h = hd(r), hd(kw), hd(vw), hd(a)
    y, s_wkv_new = _rwkv7_scan(s_wkv.astype(f32), r_h, hd(decay), k_h, v_h, -kk, kk * a_h)
    mean = jnp.mean(y, axis=-1, keepdims=True)
    var = jnp.mean(jnp.square(y - mean), axis=-1, keepdims=True)
    y = (y - mean) * lax.rsqrt(var + GN_EPS_RWKV) * gn_g.reshape(RWKV_HEADS, RWKV_HEAD) \
        + gn_b.reshape(RWKV_HEADS, RWKV_HEAD)
    y = y + jnp.sum(r_h * k_h * r_k, axis=-1, keepdims=True) * v_h
    rwkv_out = y.reshape(B, T, RWKV_W) * jax.nn.silu(g_rwkv)

    mix = jnp.concatenate([ret_out, rwkv_out], axis=-1)
    out = x + jnp.einsum('bte,ed->btd', mix, w_out).astype(x.dtype)
    return out, s_ret_new, s_wkv_new, h[:, -1]


def _layer_rglru(x, pos, h0, conv_buf, norm_g, w_in, conv_w, conv_b, wa, ba, wx, bx, lam, w_out):
    f32 = jnp.float32
    B, T, _ = x.shape
    hn = _rmsnorm(x, norm_g)
    p = jnp.einsum('btd,de->bte', hn, w_in).astype(f32)
    xb, gb = p[..., :LRU_W], p[..., LRU_W:]
    xpad = jnp.concatenate([conv_buf.astype(f32), xb], axis=1)
    xc = conv_b.astype(f32)
    for j in range(CONV_W):
        xc = xc + xpad[:, j:j + T] * conv_w[j]
    xcb = xc.reshape(B, T, LRU_BLOCKS, LRU_BW)
    gate_r = jax.nn.sigmoid(jnp.einsum('btnc,ncd->btnd', xcb, wa).reshape(B, T, LRU_W) + ba)
    gate_i = jax.nn.sigmoid(jnp.einsum('btnc,ncd->btnd', xcb, wx).reshape(B, T, LRU_W) + bx)
    log_a = -LRU_C * gate_r * jax.nn.softplus(-lam)
    a = jnp.exp(log_a)
    mult = jnp.sqrt(-jnp.expm1(2.0 * log_a))
    mult = jnp.where((pos == 0)[None, :, None], 1.0, mult)
    bterm = mult * (gate_i * xc)
    bterm = bterm.at[:, 0].add(a[:, 0] * h0.astype(f32))

    def comb(lhs, rhs):
        a1, b1 = lhs
        a2, b2 = rhs
        return a1 * a2, a2 * b1 + b2

    _, hs = lax.associative_scan(comb, (a, bterm), axis=1)
    y = hs * jax.nn.silu(gb)
    out = x + jnp.einsum('bte,ed->btd', y, w_out).astype(x.dtype)
    return out, hs[:, -1], xpad[:, -(CONV_W - 1):]


def setup_inputs(seed: int = 0) -> dict:
    key = jax.random.key(seed)
    ks = jax.random.split(key, 40)
    f32 = jnp.float32

    def nrm(i, shape, scale):
        return jax.random.normal(ks[i], shape, f32) * scale

    def uni(i, shape, lo, hi):
        return jax.random.uniform(ks[i], shape, f32, lo, hi)

    s = uni(30, (LRU_W,), 0.9, 0.999) ** (1.0 / LRU_C)
    return {
        'x_prompt': nrm(0, (BATCH, SEQ, D_MODEL), 1.0),
        'x_sample': nrm(1, (DEC_BATCH, DEC_SEQ, D_MODEL), 1.0),
        'state_ret': nrm(2, (DEC_BATCH, RET_HEADS, RET_DK, RET_DV), 1.0),
        'state_rwkv_wkv': nrm(3, (DEC_BATCH, RWKV_HEADS, RWKV_HEAD, RWKV_HEAD), 0.5),
        'state_rwkv_shift': nrm(4, (DEC_BATCH, D_MODEL), 1.0),
        'state_lru_h': nrm(5, (DEC_BATCH, LRU_W), 0.5),
        'state_lru_conv': nrm(6, (DEC_BATCH, CONV_W - 1, LRU_W), 1.0),
        'norm0_g': 1.0 + nrm(7, (D_MODEL,), 0.02),
        'w_in0': nrm(8, (D_MODEL, IN0), D_MODEL ** -0.5),
        'w_out0': nrm(9, (MIX0, D_MODEL), MIX0 ** -0.5),
        'rwkv_mu_rkv': uni(10, (3 * RWKV_W,), 0.0, 1.0),
        'rwkv_mu_w': uni(11, (D_MODEL,), 0.0, 1.0),
        'rwkv_mu_a': uni(12, (D_MODEL,), 0.0, 1.0),
        'rwkv_w0': uni(13, (RWKV_W,), -6.0, 1.0),
        'rwkv_w1': nrm(14, (D_MODEL, DECAY_LORA), D_MODEL ** -0.5),
        'rwkv_w2': nrm(15, (DECAY_LORA, RWKV_W), 0.1 * DECAY_LORA ** -0.5),
        'rwkv_a0': nrm(16, (RWKV_W,), 0.5),
        'rwkv_a1': nrm(17, (D_MODEL, ICLR_LORA), D_MODEL ** -0.5),
        'rwkv_a2': nrm(18, (ICLR_LORA, RWKV_W), 0.5 * ICLR_LORA ** -0.5),
        'rwkv_k_k': 0.85 + nrm(19, (RWKV_W,), 0.02),
        'rwkv_k_a': 1.0 + nrm(20, (RWKV_W,), 0.02),
        'rwkv_r_k': nrm(21, (RWKV_HEADS, RWKV_HEAD), 0.1),
        'rwkv_gn_g': 1.0 + nrm(22, (RWKV_W,), 0.02),
        'rwkv_gn_b': nrm(23, (RWKV_W,), 0.01),
        'norm1_g': 1.0 + nrm(24, (D_MODEL,), 0.02),
        'w_in1': nrm(25, (D_MODEL, 2 * LRU_W), D_MODEL ** -0.5),
        'lru_conv_w': nrm(26, (CONV_W, LRU_W), CONV_W ** -0.5),
        'lru_conv_b': nrm(27, (LRU_W,), 0.01),
        'lru_wa': nrm(28, (LRU_BLOCKS, LRU_BW, LRU_BW), LRU_BW ** -0.5),
        'lru_ba': nrm(29, (LRU_W,), 0.01),
        'lru_wx': nrm(31, (LRU_BLOCKS, LRU_BW, LRU_BW), LRU_BW ** -0.5),
        'lru_bx': nrm(32, (LRU_W,), 0.01),
        'lru_lambda': jnp.log(s) - jnp.log1p(-s),
        'w_out1': nrm(33, (LRU_W, D_MODEL), LRU_W ** -0.5),
        'final_g': 1.0 + nrm(34, (D_MODEL,), 0.02),
    }


def reference(x_prompt, x_sample, state_ret, state_rwkv_wkv, state_rwkv_shift, state_lru_h, state_lru_conv,
              norm0_g, w_in0, w_out0, rwkv_mu_rkv, rwkv_mu_w, rwkv_mu_a, rwkv_w0, rwkv_w1, rwkv_w2,
              rwkv_a0, rwkv_a1, rwkv_a2, rwkv_k_k, rwkv_k_a, rwkv_r_k, rwkv_gn_g, rwkv_gn_b,
              norm1_g, w_in1, lru_conv_w, lru_conv_b, lru_wa, lru_ba, lru_wx, lru_bx, lru_lambda, w_out1,
              final_g):
    f32 = jnp.float32
    bp = x_prompt.shape[0]
    pos_p = jnp.arange(x_prompt.shape[1])
    pos_s = PAST_LEN + jnp.arange(x_sample.shape[1])
    z_ret = jnp.zeros((bp, RET_HEADS, RET_DK, RET_DV), f32)
    z_wkv = jnp.zeros((bp, RWKV_HEADS, RWKV_HEAD, RWKV_HEAD), f32)
    z_shift = jnp.zeros((bp, D_MODEL), f32)
    z_h = jnp.zeros((bp, LRU_W), f32)
    z_conv = jnp.zeros((bp, CONV_W - 1, LRU_W), f32)
    ab_w = (norm0_g, w_in0, w_out0, rwkv_mu_rkv, rwkv_mu_w, rwkv_mu_a, rwkv_w0, rwkv_w1, rwkv_w2,
            rwkv_a0, rwkv_a1, rwkv_a2, rwkv_k_k, rwkv_k_a, rwkv_r_k, rwkv_gn_g, rwkv_gn_b)
    c_w = (norm1_g, w_in1, lru_conv_w, lru_conv_b, lru_wa, lru_ba, lru_wx, lru_bx, lru_lambda, w_out1)
    yp, ys = x_prompt, x_sample
    for layer in range(DEPTH):
        if layer % 2 == 0:
            yp, ret_p, wkv_p, shift_p = _layer_ret_rwkv(yp, pos_p, z_ret, z_wkv, z_shift, *ab_w)
            ys, ret_s, wkv_s, shift_s = _layer_ret_rwkv(ys, pos_s, state_ret, state_rwkv_wkv,
                                                        state_rwkv_shift, *ab_w)
        else:
            yp, lh_p, lc_p = _layer_rglru(yp, pos_p, z_h, z_conv, *c_w)
            ys, lh_s, lc_s = _layer_rglru(ys, pos_s, state_lru_h, state_lru_conv, *c_w)
    y_prompt = _rmsnorm(yp, final_g).astype(x_prompt.dtype)
    y_sample = _rmsnorm(ys, final_g).astype(x_sample.dtype)
    dp, ds = x_prompt.dtype, state_ret.dtype
    return (y_prompt, y_sample,
            ret_p.astype(dp), wkv_p.astype(dp), shift_p.astype(dp), lh_p.astype(dp), lc_p.astype(dp),
            ret_s.astype(ds), wkv_s.astype(ds), shift_s.astype(ds), lh_s.astype(ds), lc_s.astype(ds))
```

```python
import functools

import numpy as np
import jax
import jax.numpy as jnp
from jax import lax
from jax.experimental import pallas as pl
from jax.experimental.pallas import tpu as pltpu

F32 = jnp.float32
BF16 = jnp.bfloat16

D_MODEL = 2048
PAST_LEN = 16384
RET_HEADS = 8
RET_DK = 64
RET_DV = 128
RET_QK = RET_HEADS * RET_DK
RET_V = RET_HEADS * RET_DV
RET_CHUNK = 128
ROPE_BASE = 10000.0
RWKV_HEAD = 64
RWKV_HEADS = 16
RWKV_W = RWKV_HEADS * RWKV_HEAD
LORA = 64
GN_EPS_RWKV = 64e-5
LRU_W = 2048
LRU_BLOCKS = 8
LRU_BW = LRU_W // LRU_BLOCKS
CONV_W = 4
LRU_C = 8.0
NORM_EPS = 1e-6
RKV_START = 2 * RET_QK + 2 * RET_V
RKV_END = RKV_START + 3 * RWKV_W
IN0 = RKV_END + RWKV_W
LORA_START = IN0
IN0_ALL = IN0 + 512

RWKV_GROUP = 4
GROUP_W = RWKV_GROUP * RWKV_HEAD
RWKV_CHUNK = 64

VMEM_LIMIT = 48 * 1024 * 1024


def _cparams(sem):
    return pltpu.CompilerParams(dimension_semantics=sem, vmem_limit_bytes=VMEM_LIMIT)


def _split(x):
    hi = x.astype(BF16)
    lo = (x - hi.astype(F32)).astype(BF16)
    return hi, lo


_NN = (((1,), (0,)), ((), ()))
_NT = (((1,), (1,)), ((), ()))
_TN = (((0,), (0,)), ((), ()))


def _dot1(a, b, dims=_NN):
    return lax.dot_general(a.astype(BF16), b.astype(BF16), dims, preferred_element_type=F32)


def _dot3(a, b, dims=_NN):
    ah, al = _split(a)
    bh, bl = _split(b)
    d = functools.partial(lax.dot_general, dimension_numbers=dims, preferred_element_type=F32)
    return d(ah, bh) + (d(ah, bl) + d(al, bh))


def _dot_exact_rhs(a, b, dims=_NN):
    ah, al = _split(a)
    d = functools.partial(lax.dot_general, dimension_numbers=dims, preferred_element_type=F32)
    return d(ah, b) + d(al, b)


def _softplus(x):
    return jnp.maximum(x, 0.0) + jnp.log1p(jnp.exp(-jnp.abs(x)))


def _sigmoid(x):
    return 1.0 / (1.0 + jnp.exp(-x))


def _norm_matmul_kernel(x_ref, g_ref, w_ref, o_ref, h_ref, *, apply_norm):
    @pl.when(pl.program_id(1) == 0)
    def _():
        x = x_ref[...]
        if apply_norm:
            x = x * lax.rsqrt(jnp.mean(x * x, axis=-1, keepdims=True) + NORM_EPS) * g_ref[...]
        h_ref[...] = x.astype(BF16)

    o_ref[...] = jnp.dot(h_ref[...], w_ref[...], preferred_element_type=F32)


def _norm_matmul(x, g, w, *, apply_norm, tm, tn):
    m, d = x.shape
    n = w.shape[1]
    return pl.pallas_call(
        functools.partial(_norm_matmul_kernel, apply_norm=apply_norm),
        grid=(m // tm, n // tn),
        in_specs=[
            pl.BlockSpec((tm, d), lambda i, j: (i, 0)),
            pl.BlockSpec((1, d), lambda i, j: (0, 0)),
            pl.BlockSpec((d, tn), lambda i, j: (0, j)),
        ],
        out_specs=pl.BlockSpec((tm, tn), lambda i, j: (i, j)),
        out_shape=jax.ShapeDtypeStruct((m, n), F32),
        scratch_shapes=[pltpu.VMEM((tm, d), BF16)],
        compiler_params=_cparams(("parallel", "arbitrary")),
        name="norm_matmul",
    )(x, g.reshape(1, d), w)


def _rmsnorm_kernel(x_ref, g_ref, o_ref):
    x = x_ref[...]
    o_ref[...] = x * lax.rsqrt(jnp.mean(x * x, axis=-1, keepdims=True) + NORM_EPS) * g_ref[...]


def _rmsnorm_rows(x, g):
    return pl.pallas_call(
        _rmsnorm_kernel,
        out_shape=jax.ShapeDtypeStruct(x.shape, F32),
        name="rmsnorm_rows",
    )(x, g.reshape(1, -1))


def _ret_kernel(q_ref, k_ref, v_ref, g_ref, cos_ref, sin_ref, dmask_ref, qdec_ref, kdec_ref, sdec_ref,
                *rest, nb, C, has_s0):
    if has_s0:
        s0_ref, o_ref, sout_ref, s_scr = rest
    else:
        o_ref, sout_ref, s_scr = rest
    c = pl.program_id(2)

    @pl.when(c == 0)
    def _():
        if has_s0:
            s_scr[...] = s0_ref[...]
        else:
            s_scr[...] = jnp.zeros(s_scr.shape, F32)

    lane = lax.broadcasted_iota(jnp.int32, (C, 2 * RET_DK), 1)
    first_half = (lane & (RET_DK // 2)) == 0
    cos = cos_ref[...]
    sin = sin_ref[...]

    def rope(x):
        rot = jnp.where(first_half, pltpu.roll(x, 2 * RET_DK - RET_DK // 2, 1), pltpu.roll(x, RET_DK // 2, 1))
        return x * cos + rot * sin

    def one_sequence(i, carry):
        r0 = pl.multiple_of(i * C, C)
        rows = pl.ds(r0, C)
        q = rope(q_ref[rows, :])
        k = rope(k_ref[rows, :]) * (RET_DK ** -0.5)
        kd = k * kdec_ref[0]
        for h in range(2):
            qh = q[:, h * RET_DK:(h + 1) * RET_DK].astype(BF16)
            kh = k[:, h * RET_DK:(h + 1) * RET_DK].astype(BF16)
            kdh = kd[:, h * RET_DK:(h + 1) * RET_DK].astype(BF16)
            vh = v_ref[rows, h * RET_DV:(h + 1) * RET_DV].astype(BF16)
            s_old = s_scr[i, h]
            sc = lax.dot_general(qh, kh, _NT, preferred_element_type=F32) * dmask_ref[h]
            o = jnp.dot(sc.astype(BF16), vh, preferred_element_type=F32)
            o = o + jnp.dot(qh, s_old.astype(BF16), preferred_element_type=F32) * qdec_ref[h]
            s_scr[i, h] = s_old * sdec_ref[h] + lax.dot_general(kdh, vh, _TN, preferred_element_type=F32)
            o = o * lax.rsqrt(jnp.mean(o * o, axis=-1, keepdims=True) + NORM_EPS)
            gate = g_ref[rows, h * RET_DV:(h + 1) * RET_DV]
            o_ref[rows, h * RET_DV:(h + 1) * RET_DV] = (o * (gate * _sigmoid(gate))).astype(BF16)
        return carry

    if nb == 1:
        one_sequence(0, 0)
    else:
        lax.fori_loop(0, nb, one_sequence, 0)

    @pl.when(c == pl.num_programs(2) - 1)
    def _():
        sout_ref[...] = s_scr[...]


def _ret_tables(pos, C):
    inv = 1.0 / (ROPE_BASE ** (np.arange(0, RET_DK, 2, dtype=np.float64) / RET_DK))
    ang = pos.astype(np.float64)[:, None] * inv[None, :]
    cos = np.tile(np.cos(ang), (1, 4))
    sin = np.tile(np.concatenate([-np.sin(ang), np.sin(ang)], axis=1), (1, 2))
    log_gamma = np.log(1.0 - 2.0 ** (-5.0 - np.arange(RET_HEADS, dtype=np.float64)))
    n = np.arange(C, dtype=np.float64)
    diff = n[:, None] - n[None, :]
    dmask = np.where(diff >= 0, np.exp(log_gamma[:, None, None] * np.maximum(diff, 0.0)), 0.0)
    qdec = np.broadcast_to(np.exp(log_gamma[:, None] * (n + 1.0))[:, :, None], (RET_HEADS, C, RET_DV))
    kdec = np.exp(log_gamma[:, None] * (C - 1.0 - n))
    kdec = np.repeat(kdec.reshape(RET_HEADS // 2, 2, C).transpose(0, 2, 1), RET_DK, axis=2)
    sdec = np.broadcast_to(np.exp(log_gamma * C)[:, None, None], (RET_HEADS, 1, RET_DV))
    f = lambda a: jnp.asarray(np.ascontiguousarray(a), F32)
    return f(cos), f(sin), f(dmask), f(qdec), f(kdec), f(sdec)


def _retention(p0, B, T, C, nb, tables, s0):
    cos, sin, dmask, qdec, kdec, sdec = tables
    nc = T // C
    rows = nb * C
    hp = RET_HEADS // 2
    row_map = lambda off: (lambda b, h, c: (b * nc + c, off + h))
    in_specs = [
        pl.BlockSpec((rows, 2 * RET_DK), row_map(0)),
        pl.BlockSpec((rows, 2 * RET_DK), row_map(RET_QK // (2 * RET_DK))),
        pl.BlockSpec((rows, 2 * RET_DV), row_map(2 * RET_QK // (2 * RET_DV))),
        pl.BlockSpec((rows, 2 * RET_DV), row_map((2 * RET_QK + RET_V) // (2 * RET_DV))),
        pl.BlockSpec((C, 2 * RET_DK), lambda b, h, c: (c, 0)),
        pl.BlockSpec((C, 2 * RET_DK), lambda b, h, c: (c, 0)),
        pl.BlockSpec((2, C, C), lambda b, h, c: (h, 0, 0)),
        pl.BlockSpec((2, C, RET_DV), lambda b, h, c: (h, 0, 0)),
        pl.BlockSpec((1, C, 2 * RET_DK), lambda b, h, c: (h, 0, 0)),
        pl.BlockSpec((2, 1, RET_DV), lambda b, h, c: (h, 0, 0)),
    ]
    args = [p0, p0, p0, p0, cos, sin, dmask, qdec, kdec, sdec]
    state_spec = pl.BlockSpec((nb, 2, RET_DK, RET_DV), lambda b, h, c: (b, h, 0, 0))
    if s0 is not None:
        in_specs.append(state_spec)
        args.append(s0)
    return pl.pallas_call(
        functools.partial(_ret_kernel, nb=nb, C=C, has_s0=s0 is not None),
        grid=(B // nb, hp, nc),
        in_specs=in_specs,
        out_specs=[pl.BlockSpec((rows, 2 * RET_DV), lambda b, h, c: (b * nc + c, h)), state_spec],
        out_shape=[jax.ShapeDtypeStruct((B * T, RET_V), BF16),
                   jax.ShapeDtypeStruct((B, RET_HEADS, RET_DK, RET_DV), F32)],
        scratch_shapes=[pltpu.VMEM((nb, 2, RET_DK, RET_DV), F32)],
        compiler_params=_cparams(("parallel", "parallel", "arbitrary")),
        name="retention",
    )(*args)


def _rwkv_kernel(pr_ref, pk_ref, pv_ref, pg_ref, pl_ref, mur_ref, muk_ref, muv_ref, w0_ref, a0_ref, kk_ref, ka_ref,
                 rk_ref, gng_ref, gnb_ref, w2_ref, a2_ref, tri_ref, ones_ref, *rest, nb, C, has_state):
    if has_state:
        fr_ref, fk_ref, fv_ref, fl_ref, s0_ref, o_ref, sout_ref, s_scr, prev_scr = rest
    else:
        o_ref, sout_ref, s_scr, prev_scr = rest
    c = pl.program_id(2)
    G = RWKV_GROUP
    GC = G * C
    log2c = C.bit_length() - 1
    log2n = RWKV_HEAD.bit_length() - 1
    pair_w = 2 * RWKV_HEAD

    @pl.when(c == 0)
    def _():
        s_scr[...] = jnp.zeros(s_scr.shape, F32)
        prev_scr[...] = jnp.zeros(prev_scr.shape, F32)
        if has_state:
            lane = lax.broadcasted_iota(jnp.int32, (RWKV_HEAD, pair_w), 1)
            for i in range(nb):
                for p in range(2):
                    pair = s0_ref[i, p]
                    blk = jnp.concatenate([jnp.where(lane < RWKV_HEAD, pair, 0.0),
                                           jnp.where(lane >= RWKV_HEAD, pair, 0.0)], axis=0)
                    s_scr[i, p * pair_w:(p + 1) * pair_w, p * pair_w:(p + 1) * pair_w] = blk

    row = lax.broadcasted_iota(jnp.int32, (C, GROUP_W), 0)
    is_row0 = row == 0
    t_idx = lax.broadcasted_iota(jnp.int32, (C, GC), 0)
    s_idx = lax.broadcasted_iota(jnp.int32, (C, GC), 1) & (C - 1)
    strict = s_idx < t_idx
    incl = s_idx <= t_idx
    eye_lb = jnp.where(s_idx == t_idx, 1.0, 0.0).astype(F32)
    bd_feat = (lax.broadcasted_iota(jnp.int32, (GC, GROUP_W), 0) >> log2c) == \
              (lax.broadcasted_iota(jnp.int32, (GC, GROUP_W), 1) >> log2n)
    bd_time = (lax.broadcasted_iota(jnp.int32, (GC, GC), 0) >> log2c) == \
              (lax.broadcasted_iota(jnp.int32, (GC, GC), 1) >> log2c)
    bd_state = (lax.broadcasted_iota(jnp.int32, (GROUP_W, GROUP_W), 0) >> log2n) == \
               (lax.broadcasted_iota(jnp.int32, (GROUP_W, GROUP_W), 1) >> log2n)

    def expand(y, mask):
        return jnp.where(mask, jnp.concatenate([y] * G, axis=0), 0.0)

    ones_bd = ones_ref[...]

    def seg_sum(x):
        return _dot_exact_rhs(x, ones_bd)

    def one_sequence(i, carry):
        r0 = pl.multiple_of(i * C, C)
        rows = pl.ds(r0, C)

        def shifted(x, k, first_ref):
            if has_state:
                prev = first_ref[pl.ds(i, 1), :]
            else:
                prev = prev_scr[k]
            return jnp.where(is_row0, prev, pltpu.roll(x, 1, 0))

        pr = pr_ref[rows, :]
        pk = pk_ref[rows, :]
        pv = pv_ref[rows, :]
        plo = pl_ref[rows, :]
        pr_s = shifted(pr, 0, fr_ref if has_state else None)
        pk_s = shifted(pk, 1, fk_ref if has_state else None)
        pv_s = shifted(pv, 2, fv_ref if has_state else None)
        plo_s = shifted(plo, 3, fl_ref if has_state else None)
        if not has_state:
            prev_scr[0] = pr[C - 1:C, :]
            prev_scr[1] = pk[C - 1:C, :]
            prev_scr[2] = pv[C - 1:C, :]
            prev_scr[3] = plo[C - 1:C, :]

        r = pr + mur_ref[...] * (pr_s - pr)
        kw = pk + muk_ref[...] * (pk_s - pk)
        vw = pv + muv_ref[...] * (pv_s - pv)
        pre_w = plo[:, 0:LORA] + plo_s[:, LORA:2 * LORA]
        pre_a = plo[:, 2 * LORA:3 * LORA] + plo_s[:, 3 * LORA:4 * LORA]
        wl = _dot1(jnp.tanh(pre_w), w2_ref[...])
        al = _dot1(pre_a, a2_ref[...])
        w_log = -_softplus(-(w0_ref[...] + wl)) - 0.5
        logw = -jnp.exp(w_log)
        a = _sigmoid(a0_ref[...] + al)
        kk = kw * kk_ref[...]
        kk = kk / jnp.maximum(jnp.sqrt(seg_sum(kk * kk)), 1e-12)
        k = kw * (1.0 + (a - 1.0) * ka_ref[...])
        av = -kk
        bv = kk * a

        cum = _dot_exact_rhs_lhs(tri_ref[...], logw)
        tot = cum[C - 1:C, :]
        e_in = jnp.exp(cum)
        e_neg = jnp.exp(-cum)
        a_t = av * jnp.exp(cum - logw)
        r_t = r * e_in
        b_t = bv * e_neg
        k_t = k * e_neg
        e_rem = jnp.exp(tot - cum)
        b_h = bv * e_rem
        k_h = k * e_rem

        s_old = s_scr[i]
        ar = jnp.concatenate([a_t, r_t], axis=0)
        xb = _dot3(ar, expand(b_t, bd_feat), _NT)
        xk = _dot3(ar, expand(k_t, bd_feat), _NT)
        xs = _dot3(ar, s_old, _NT)
        l_ab = jnp.where(strict, xb[:C], 0.0)
        m_rb = jnp.where(incl, xb[C:], 0.0)
        l_ak = jnp.where(strict, xk[:C], 0.0)
        m_rk = jnp.where(incl, xk[C:], 0.0)
        v_bd = expand(vw, bd_feat)
        rhs = xs[:C] + _dot3(l_ak, v_bd)
        p = eye_lb + l_ab
        q = _dot3(l_ab, expand(l_ab, bd_time))
        for _ in range(log2c - 1):
            pq = _dot3(jnp.concatenate([p, q], axis=0), expand(q, bd_time))
            p = p + pq[:C]
            q = pq[C:]
        u = _dot3(p, expand(rhs, bd_feat))
        y = xs[C:] + _dot3(m_rb, expand(u, bd_feat)) + _dot3(m_rk, v_bd)
        s_add = _dot3(u, b_h, _TN) + _dot3(vw, k_h, _TN)
        s_scr[i] = s_old * jnp.exp(tot) + jnp.where(bd_state, s_add, 0.0)

        inv_n = 1.0 / RWKV_HEAD
        mean = seg_sum(y) * inv_n
        d = y - mean
        var = seg_sum(d * d) * inv_n
        yn = d * lax.rsqrt(var + GN_EPS_RWKV) * gng_ref[...] + gnb_ref[...]
        yn = yn + seg_sum(r * k * rk_ref[...]) * vw
        gate = pg_ref[rows, :]
        o_ref[rows, :] = (yn * (gate * _sigmoid(gate))).astype(BF16)
        return carry

    if nb == 1:
        one_sequence(0, 0)
    else:
        lax.fori_loop(0, nb, one_sequence, 0)

    @pl.when(c == pl.num_programs(2) - 1)
    def _():
        for i in range(nb):
            for p in range(2):
                lo = p * pair_w
                sout_ref[i, p] = (s_scr[i, lo:lo + RWKV_HEAD, lo:lo + pair_w]
                                  + s_scr[i, lo + RWKV_HEAD:lo + pair_w, lo:lo + pair_w])


def _dot_exact_rhs_lhs(tri, x):
    xh, xl = _split(x)
    return jnp.dot(tri, xh, preferred_element_type=F32) + jnp.dot(tri, xl, preferred_element_type=F32)


def _rwkv(p0, B, T, C, nb, pfirst, s0_pairs, prm):
    nc = T // C
    rows = nb * C
    ng = RWKV_HEADS // RWKV_GROUP
    gw = GROUP_W
    col = lambda off: (lambda b, g, c: (b * nc + c, off // gw + g))
    vec = lambda off: pl.BlockSpec((1, gw), lambda b, g, c: (0, off // gw + g))
    in_specs = [
        pl.BlockSpec((rows, gw), col(RKV_START)),
        pl.BlockSpec((rows, gw), col(RKV_START + RWKV_W)),
        pl.BlockSpec((rows, gw), col(RKV_START + 2 * RWKV_W)),
        pl.BlockSpec((rows, gw), col(RKV_END)),
        pl.BlockSpec((rows, gw), lambda b, g, c: (b * nc + c, LORA_START // gw)),
        vec(0), vec(RWKV_W), vec(2 * RWKV_W),
        vec(0), vec(0), vec(0), vec(0), vec(0), vec(0), vec(0),
        pl.BlockSpec((LORA, gw), lambda b, g, c: (0, g)),
        pl.BlockSpec((LORA, gw), lambda b, g, c: (0, g)),
        pl.BlockSpec((C, C), lambda b, g, c: (0, 0)),
        pl.BlockSpec((gw, gw), lambda b, g, c: (0, 0)),
    ]
    tri = jnp.asarray(np.tril(np.ones((C, C), np.float32)), BF16)
    seg = np.arange(gw) // RWKV_HEAD
    ones_bd = jnp.asarray((seg[:, None] == seg[None, :]).astype(np.float32), BF16)
    mu = prm["mu_rkv"].reshape(1, -1)
    args = [p0, p0, p0, p0, p0, mu, mu, mu, prm["w0"], prm["a0"], prm["k_k"], prm["k_a"], prm["r_k"],
            prm["gn_g"], prm["gn_b"], prm["w2"], prm["a2"], tri, ones_bd]
    state_spec = pl.BlockSpec((nb, 2, RWKV_HEAD, 2 * RWKV_HEAD), lambda b, g, c: (b, g, 0, 0))
    has_state = s0_pairs is not None
    if has_state:
        fcol = lambda off: pl.BlockSpec((nb, gw), lambda b, g, c: (b, off // gw + g))
        in_specs += [fcol(RKV_START), fcol(RKV_START + RWKV_W), fcol(RKV_START + 2 * RWKV_W),
                     pl.BlockSpec((nb, gw), lambda b, g, c: (b, LORA_START // gw)), state_spec]
        args += [pfirst, pfirst, pfirst, pfirst, s0_pairs]
    return pl.pallas_call(
        functools.partial(_rwkv_kernel, nb=nb, C=C, has_state=has_state),
        grid=(B // nb, ng, nc),
        in_specs=in_specs,
        out_specs=[pl.BlockSpec((rows, gw), lambda b, g, c: (b * nc + c, g)), state_spec],
        out_shape=[jax.ShapeDtypeStruct((B * T, RWKV_W), BF16),
                   jax.ShapeDtypeStruct((B, RWKV_HEADS // 2, RWKV_HEAD, 2 * RWKV_HEAD), F32)],
        scratch_shapes=[pltpu.VMEM((nb, gw, gw), F32), pltpu.VMEM((4, 1, gw), F32)],
        compiler_params=_cparams(("parallel", "parallel", "arbitrary")),
        name="rwkv7",
    )(*args)


def _outproj0_kernel(x_ref, a1_ref, a2_ref, w1_ref, w2_ref, o_ref):
    acc = jnp.dot(a1_ref[...], w1_ref[...], preferred_element_type=F32)
    acc = acc + jnp.dot(a2_ref[...], w2_ref[...], preferred_element_type=F32)
    o_ref[...] = x_ref[...] + acc


def _outproj0(x, a1, a2, w, *, tm, tn):
    m, d = x.shape
    k1 = a1.shape[1]
    k2 = a2.shape[1]
    return pl.pallas_call(
        _outproj0_kernel,
        grid=(m // tm, d // tn),
        in_specs=[
            pl.BlockSpec((tm, tn), lambda i, j: (i, j)),
            pl.BlockSpec((tm, k1), lambda i, j: (i, 0)),
            pl.BlockSpec((tm, k2), lambda i, j: (i, 0)),
            pl.BlockSpec((k1, tn), lambda i, j: (0, j)),
            pl.BlockSpec((k2, tn), lambda i, j: (k1 // k2, j)),
        ],
        out_specs=pl.BlockSpec((tm, tn), lambda i, j: (i, j)),
        out_shape=jax.ShapeDtypeStruct((m, d), F32),
        compiler_params=_cparams(("parallel", "arbitrary")),
        name="outproj0",
    )(x, a1, a2, w, w)


def _outproj1_kernel(x_ref, y_ref, w_ref, g_ref, o_ref):
    out = x_ref[...] + jnp.dot(y_ref[...], w_ref[...], preferred_element_type=F32)
    o_ref[...] = out * lax.rsqrt(jnp.mean(out * out, axis=-1, keepdims=True) + NORM_EPS) * g_ref[...]


def _outproj1_norm(x, y, w, g, *, tm):
    m, d = x.shape
    k = y.shape[1]
    return pl.pallas_call(
        _outproj1_kernel,
        grid=(m // tm,),
        in_specs=[
            pl.BlockSpec((tm, d), lambda i: (i, 0)),
            pl.BlockSpec((tm, k), lambda i: (i, 0)),
            pl.BlockSpec((k, d), lambda i: (0, 0)),
            pl.BlockSpec((1, d), lambda i: (0, 0)),
        ],
        out_specs=pl.BlockSpec((tm, d), lambda i: (i, 0)),
        out_shape=jax.ShapeDtypeStruct((m, d), F32),
        compiler_params=_cparams(("parallel",)),
        name="outproj1_norm",
    )(x, y, w, g.reshape(1, d))


def _lru_gates(xc, wa_ref, wx_ref, ba, bx, spl):
    xcb = xc.astype(BF16)
    ra, rx = [], []
    for n in range(LRU_BLOCKS):
        blk = xcb[:, n * LRU_BW:(n + 1) * LRU_BW]
        ra.append(jnp.dot(blk, wa_ref[n], preferred_element_type=F32))
        rx.append(jnp.dot(blk, wx_ref[n], preferred_element_type=F32))
    gate_r = _sigmoid(jnp.concatenate(ra, axis=1) + ba)
    gate_i = _sigmoid(jnp.concatenate(rx, axis=1) + bx)
    log_a = -LRU_C * gate_r * spl
    a = jnp.exp(log_a)
    mult = jnp.sqrt(-jnp.tanh(log_a) * (a * a + 1.0))
    return a, mult, gate_i * xc


def _lru_prompt_kernel(xb_ref, gb_ref, cw_ref, cb_ref, wa_ref, wx_ref, ba_ref, bx_ref, lam_ref,
                       y_ref, hl_ref, xp_scr, a_scr, b_scr, h_scr, *, tt):
    i = pl.program_id(1)
    pad = 8

    @pl.when(i == 0)
    def _():
        xp_scr[0:pad, :] = jnp.zeros((pad, LRU_W), F32)
        h_scr[...] = jnp.zeros(h_scr.shape, F32)

    xp_scr[pad:pad + tt, :] = xb_ref[...]
    xc = cb_ref[...] + xp_scr[pad - 3:pad - 3 + tt, :] * cw_ref[0:1, :]
    for j in range(1, CONV_W):
        xc = xc + xp_scr[pad - 3 + j:pad - 3 + j + tt, :] * cw_ref[j:j + 1, :]
    xp_scr[0:pad, :] = xp_scr[tt:tt + pad, :]

    spl = _softplus(-lam_ref[...])
    a, mult, gx = _lru_gates(xc, wa_ref, wx_ref, ba_ref[...], bx_ref[...], spl)
    row = lax.broadcasted_iota(jnp.int32, (tt, LRU_W), 0)
    mult = jnp.where((row == 0) & (i == 0), 1.0, mult)
    a_scr[...] = a
    b_scr[...] = mult * gx

    def body(s, h):
        r0 = pl.multiple_of(s * 8, 8)
        a8 = a_scr[pl.ds(r0, 8), :]
        b8 = b_scr[pl.ds(r0, 8), :]
        out = []
        for j in range(8):
            h = a8[j:j + 1, :] * h + b8[j:j + 1, :]
            out.append(h)
        b_scr[pl.ds(r0, 8), :] = jnp.concatenate(out, axis=0)
        return h

    h = lax.fori_loop(0, tt // 8, body, h_scr[...])
    h_scr[...] = h
    gb = gb_ref[...]
    y_ref[...] = (b_scr[...] * (gb * _sigmoid(gb))).astype(BF16)

    @pl.when(i == pl.num_programs(1) - 1)
    def _():
        hl_ref[0] = h


def _lru_prompt(p1, B, T, prm, *, tt):
    nt = T // tt
    vec = pl.BlockSpec((1, LRU_W), lambda b, i: (0, 0))
    wspec = pl.BlockSpec((LRU_BLOCKS, LRU_BW, LRU_BW), lambda b, i: (0, 0, 0))
    return pl.pallas_call(
        functools.partial(_lru_prompt_kernel, tt=tt),
        grid=(B, nt),
        in_specs=[
            pl.BlockSpec((tt, LRU_W), lambda b, i: (b * nt + i, 0)),
            pl.BlockSpec((tt, LRU_W), lambda b, i: (b * nt + i, 1)),
            pl.BlockSpec((CONV_W, LRU_W), lambda b, i: (0, 0)),
            vec, wspec, wspec, vec, vec, vec,
        ],
        out_specs=[pl.BlockSpec((tt, LRU_W), lambda b, i: (b * nt + i, 0)),
                   pl.BlockSpec((1, 1, LRU_W), lambda b, i: (b, 0, 0))],
        out_shape=[jax.ShapeDtypeStruct((B * T, LRU_W), BF16), jax.ShapeDtypeStruct((B, 1, LRU_W), F32)],
        scratch_shapes=[pltpu.VMEM((tt + 8, LRU_W), F32), pltpu.VMEM((tt, LRU_W), F32),
                        pltpu.VMEM((tt, LRU_W), F32), pltpu.VMEM((1, LRU_W), F32)],
        compiler_params=_cparams(("parallel", "arbitrary")),
        name="lru_prompt",
    )(p1, p1, prm["conv_w"], prm["conv_b"], prm["wa"], prm["wx"], prm["ba"], prm["bx"], prm["lam"])


def _lru_sample_kernel(xb_ref, gb_ref, cbuf_ref, h0_ref, cw_ref, cb_ref, wa_ref, wx_ref, ba_ref, bx_ref, lam_ref,
                       y_ref, hl_ref, *, T):
    spl = _softplus(-lam_ref[...])
    h = h0_ref[...]
    for t in range(T):
        xc = cb_ref[...]
        for j in range(CONV_W):
            src = t + j - (CONV_W - 1)
            tap = xb_ref[src] if src >= 0 else cbuf_ref[src + CONV_W - 1]
            xc = xc + tap * cw_ref[j:j + 1, :]
        a, mult, gx = _lru_gates(xc, wa_ref, wx_ref, ba_ref[...], bx_ref[...], spl)
        h = a * h + mult * gx
        gb = gb_ref[t]
        y_ref[t] = (h * (gb * _sigmoid(gb))).astype(BF16)
    hl_ref[...] = h


def _lru_sample(p1_tm, cbuf_tm, h0, prm, *, bt):
    T, B, _ = p1_tm.shape
    vec = pl.BlockSpec((1, LRU_W), lambda i: (0, 0))
    wspec = pl.BlockSpec((LRU_BLOCKS, LRU_BW, LRU_BW), lambda i: (0, 0, 0))
    return pl.pallas_call(
        functools.partial(_lru_sample_kernel, T=T),
        grid=(B // bt,),
        in_specs=[
            pl.BlockSpec((T, bt, LRU_W), lambda i: (0, i, 0)),
            pl.BlockSpec((T, bt, LRU_W), lambda i: (0, i, 1)),
            pl.BlockSpec((CONV_W - 1, bt, LRU_W), lambda i: (0, i, 0)),
            pl.BlockSpec((bt, LRU_W), lambda i: (i, 0)),
            pl.BlockSpec((CONV_W, LRU_W), lambda i: (0, 0)),
            vec, wspec, wspec, vec, vec, vec,
        ],
        out_specs=[pl.BlockSpec((T, bt, LRU_W), lambda i: (0, i, 0)), pl.BlockSpec((bt, LRU_W), lambda i: (i, 0))],
        out_shape=[jax.ShapeDtypeStruct((T, B, LRU_W), BF16), jax.ShapeDtypeStruct((B, LRU_W), F32)],
        compiler_params=_cparams(("parallel",)),
        name="lru_sample",
    )(p1_tm, p1_tm, cbuf_tm, h0, prm["conv_w"], prm["conv_b"], prm["wa"], prm["wx"], prm["ba"], prm["bx"], prm["lam"])


def _pairs_from_heads(s):
    b = s.shape[0]
    return s.reshape(b, RWKV_HEADS // 2, 2, RWKV_HEAD, RWKV_HEAD).transpose(0, 1, 3, 2, 4).reshape(
        b, RWKV_HEADS // 2, RWKV_HEAD, 2 * RWKV_HEAD)


def _heads_from_pairs(s):
    b = s.shape[0]
    return s.reshape(b, RWKV_HEADS // 2, RWKV_HEAD, 2, RWKV_HEAD).transpose(0, 1, 3, 2, 4).reshape(
        b, RWKV_HEADS, RWKV_HEAD, RWKV_HEAD)


def kernel(x_prompt, x_sample, state_ret, state_rwkv_wkv, state_rwkv_shift, state_lru_h, state_lru_conv,
           norm0_g, w_in0, w_out0, rwkv_mu_rkv, rwkv_mu_w, rwkv_mu_a, rwkv_w0, rwkv_w1, rwkv_w2,
           rwkv_a0, rwkv_a1, rwkv_a2, rwkv_k_k, rwkv_k_a, rwkv_r_k, rwkv_gn_g, rwkv_gn_b,
           norm1_g, w_in1, lru_conv_w, lru_conv_b, lru_wa, lru_ba, lru_wx, lru_bx, lru_lambda, w_out1,
           final_g):
    bp, tp, d = x_prompt.shape
    bs, ts, _ = x_sample.shape
    xp = x_prompt.reshape(bp * tp, d)
    xs = x_sample.reshape(bs * ts, d)

    mw = rwkv_mu_w[:, None]
    ma = rwkv_mu_a[:, None]
    w0_all = jnp.concatenate(
        [w_in0, (1.0 - mw) * rwkv_w1, mw * rwkv_w1, (1.0 - ma) * rwkv_a1, ma * rwkv_a1,
         jnp.zeros((d, IN0_ALL - IN0 - 4 * LORA), F32)], axis=1).astype(BF16)
    w_out0_b = w_out0.astype(BF16)
    w_in1_b = w_in1.astype(BF16)
    w_out1_b = w_out1.astype(BF16)
    row = lambda v: v.reshape(1, -1)
    rwkv_prm = dict(mu_rkv=rwkv_mu_rkv, w0=row(rwkv_w0), a0=row(rwkv_a0), k_k=row(rwkv_k_k), k_a=row(rwkv_k_a),
                    r_k=row(rwkv_r_k), gn_g=row(rwkv_gn_g), gn_b=row(rwkv_gn_b),
                    w2=rwkv_w2.astype(BF16), a2=rwkv_a2.astype(BF16))
    lru_prm = dict(conv_w=lru_conv_w, conv_b=row(lru_conv_b), wa=lru_wa.astype(BF16), wx=lru_wx.astype(BF16),
                   ba=row(lru_ba), bx=row(lru_bx), lam=row(lru_lambda))

    p0p = _norm_matmul(xp, norm0_g, w0_all, apply_norm=True, tm=512, tn=512)
    p0s = _norm_matmul(xs, norm0_g, w0_all, apply_norm=True, tm=512, tn=512)
    pfirst = _norm_matmul(state_rwkv_shift, norm0_g, w0_all, apply_norm=False, tm=bs, tn=512)

    cp = RET_CHUNK if tp % RET_CHUNK == 0 else tp
    cs = RET_CHUNK if ts % RET_CHUNK == 0 else ts
    ret_p, new_ret_p = _retention(p0p, bp, tp, cp, 1, _ret_tables(np.arange(tp), cp), None)
    ret_s, new_ret_s = _retention(p0s, bs, ts, cs, 8, _ret_tables(PAST_LEN + np.arange(ts), cs), state_ret)

    cwp = RWKV_CHUNK if tp % RWKV_CHUNK == 0 else tp
    rw_p, wkv_p = _rwkv(p0p, bp, tp, cwp, 1, None, None, rwkv_prm)
    rw_s, wkv_s = _rwkv(p0s, bs, ts, ts, 8, pfirst, _pairs_from_heads(state_rwkv_wkv), rwkv_prm)

    x1p = _outproj0(xp, ret_p, rw_p, w_out0_b, tm=512, tn=1024)
    x1s = _outproj0(xs, ret_s, rw_s, w_out0_b, tm=512, tn=1024)

    last_rows = jnp.concatenate([x_prompt[:, -1, :], x_sample[:, -1, :]], axis=0)
    shift = _rmsnorm_rows(last_rows, norm0_g)

    p1p = _norm_matmul(x1p, norm1_g, w_in1_b, apply_norm=True, tm=512, tn=512)
    x1s_tm = x1s.reshape(bs, ts, d).transpose(1, 0, 2).reshape(ts * bs, d)
    p1s = _norm_matmul(x1s_tm, norm1_g, w_in1_b, apply_norm=True, tm=512, tn=512)

    yp, lh_p = _lru_prompt(p1p, bp, tp, lru_prm, tt=256)
    ys_tm, lh_s = _lru_sample(p1s.reshape(ts, bs, 2 * LRU_W), state_lru_conv.transpose(1, 0, 2), state_lru_h,
                              lru_prm, bt=64)

    out_p = _outproj1_norm(x1p, yp, w_out1_b, final_g, tm=256)
    out_s_tm = _outproj1_norm(x1s_tm, ys_tm.reshape(ts * bs, LRU_W), w_out1_b, final_g, tm=256)

    y_prompt = out_p.reshape(bp, tp, d)
    y_sample = out_s_tm.reshape(ts, bs, d).transpose(1, 0, 2)
    lc_p = p1p.reshape(bp, tp, 2 * LRU_W)[:, tp - (CONV_W - 1):, :LRU_W]
    lc_s = p1s.reshape(ts, bs, 2 * LRU_W)[ts - (CONV_W - 1):, :, :LRU_W].transpose(1, 0, 2)
    return (y_prompt, y_sample,
            new_ret_p, _heads_from_pairs(wkv_p), shift[:bp], lh_p.reshape(bp, LRU_W), lc_p,
            new_ret_s, _heads_from_pairs(wkv_s), shift[bp:], lh_s, lc_s)
```

```python
import functools

import numpy as np
import jax
import jax.numpy as jnp
from jax import lax
from jax.experimental import pallas as pl
from jax.experimental.pallas import tpu as pltpu

F32 = jnp.float32
BF16 = jnp.bfloat16

D_MODEL = 2048
PAST_LEN = 16384
RET_HEADS = 8
RET_DK = 64
RET_DV = 128
RET_QK = RET_HEADS * RET_DK
RET_V = RET_HEADS * RET_DV
RET_CHUNK = 128
ROPE_BASE = 10000.0
RWKV_HEAD = 64
RWKV_HEADS = 16
RWKV_W = RWKV_HEADS * RWKV_HEAD
LORA = 64
GN_EPS_RWKV = 64e-5
LRU_W = 2048
LRU_BLOCKS = 8
LRU_BW = LRU_W // LRU_BLOCKS
CONV_W = 4
LRU_C = 8.0
NORM_EPS = 1e-6
RKV_START = 2 * RET_QK + 2 * RET_V
RKV_END = RKV_START + 3 * RWKV_W
IN0 = RKV_END + RWKV_W

RWKV_GROUP = 4
GROUP_W = RWKV_GROUP * RWKV_HEAD
RWKV_CHUNK = 64

VMEM_LIMIT = 48 * 1024 * 1024


def _tile(m, pref):
    return pref if m % pref == 0 else m


def _round_robin(gens):
    gens = list(gens)
    while gens:
        alive = []
        for g in gens:
            try:
                next(g)
                alive.append(g)
            except StopIteration:
                pass
        gens = alive


def _cparams(sem):
    return pltpu.CompilerParams(dimension_semantics=sem, vmem_limit_bytes=VMEM_LIMIT)


def _split(x):
    hi = x.astype(BF16)
    lo = (x - hi.astype(F32)).astype(BF16)
    return hi, lo


_NN = (((1,), (0,)), ((), ()))
_NT = (((1,), (1,)), ((), ()))
_TN = (((0,), (0,)), ((), ()))


def _dot1(a, b, dims=_NN):
    return lax.dot_general(a.astype(BF16), b.astype(BF16), dims, preferred_element_type=F32)


def _dot_exact_rhs(a, b, dims=_NN):
    ah, al = _split(a)
    d = functools.partial(lax.dot_general, dimension_numbers=dims, preferred_element_type=F32)
    return d(ah, b) + d(al, b)


def _softplus(x):
    return jnp.maximum(x, 0.0) + jnp.log1p(jnp.exp(-jnp.abs(x)))


def _sigmoid(x):
    return 1.0 / (1.0 + jnp.exp(-x))


def _norm_matmul_kernel(x_ref, g_ref, w_ref, *rest, apply_norm, has_extra):
    if has_extra:
        we_ref, o_ref, oe_ref, h_ref = rest
    else:
        o_ref, h_ref = rest

    @pl.when(pl.program_id(1) == 0)
    def _():
        x = x_ref[...]
        if apply_norm:
            x = x * lax.rsqrt(jnp.mean(x * x, axis=-1, keepdims=True) + NORM_EPS) * g_ref[...]
        h_ref[...] = x.astype(BF16)
        if has_extra:
            oe_ref[...] = jnp.dot(h_ref[...], we_ref[...], preferred_element_type=F32)

    o_ref[...] = jnp.dot(h_ref[...], w_ref[...], preferred_element_type=F32)


def _norm_matmul(x, g, w, w_extra=None, *, apply_norm, tm, tn):
    m, d = x.shape
    n = w.shape[1]
    has_extra = w_extra is not None
    in_specs = [
        pl.BlockSpec((tm, d), lambda i, j: (i, 0)),
        pl.BlockSpec((1, d), lambda i, j: (0, 0)),
        pl.BlockSpec((d, tn), lambda i, j: (0, j)),
    ]
    out_specs = [pl.BlockSpec((tm, tn), lambda i, j: (i, j))]
    out_shape = [jax.ShapeDtypeStruct((m, n), F32)]
    args = [x, g.reshape(1, d), w]
    if has_extra:
        ne = w_extra.shape[1]
        in_specs.append(pl.BlockSpec((d, ne), lambda i, j: (0, 0)))
        out_specs.append(pl.BlockSpec((tm, ne), lambda i, j: (i, 0)))
        out_shape.append(jax.ShapeDtypeStruct((m, ne), F32))
        args.append(w_extra)
    out = pl.pallas_call(
        functools.partial(_norm_matmul_kernel, apply_norm=apply_norm, has_extra=has_extra),
        grid=(m // tm, n // tn),
        in_specs=in_specs,
        out_specs=out_specs,
        out_shape=out_shape,
        scratch_shapes=[pltpu.VMEM((tm, d), BF16)],
        compiler_params=_cparams(("parallel", "arbitrary")),
        name="norm_matmul",
    )(*args)
    return out if has_extra else out[0]


def _rmsnorm_kernel(x_ref, g_ref, o_ref):
    x = x_ref[...]
    o_ref[...] = x * lax.rsqrt(jnp.mean(x * x, axis=-1, keepdims=True) + NORM_EPS) * g_ref[...]


def _rmsnorm_rows(x, g):
    return pl.pallas_call(
        _rmsnorm_kernel,
        out_shape=jax.ShapeDtypeStruct(x.shape, F32),
        name="rmsnorm_rows",
    )(x, g.reshape(1, -1))


def _ret_kernel(q_ref, k_ref, v_ref, g_ref, cos_ref, sin_ref, dmask_ref, qdec_ref, kdec_ref, sdec_ref,
                *rest, nb, C, unroll, has_s0):
    if has_s0:
        s0_ref, o_ref, sout_ref, s_scr = rest
    else:
        o_ref, sout_ref, s_scr = rest
    c = pl.program_id(1)

    @pl.when(c == 0)
    def _():
        if has_s0:
            s_scr[...] = s0_ref[...]
        else:
            s_scr[...] = jnp.zeros(s_scr.shape, F32)

    pw = 2 * RET_DK
    lane = lax.broadcasted_iota(jnp.int32, (C, pw), 1)
    first_half = (lane & (RET_DK // 2)) == 0
    cos = cos_ref[...]
    sin = sin_ref[...]

    def rope(x):
        rot = jnp.where(first_half, pltpu.roll(x, pw - RET_DK // 2, 1), pltpu.roll(x, RET_DK // 2, 1))
        return x * cos + rot * sin

    def chain(i, hp):
        q = rope(q_ref[i, :, hp * pw:(hp + 1) * pw])
        k = rope(k_ref[i, :, hp * pw:(hp + 1) * pw]) * (RET_DK ** -0.5)
        kd = k * kdec_ref[hp]
        part = []
        for h2 in range(2):
            h = 2 * hp + h2
            qh = q[:, h2 * RET_DK:(h2 + 1) * RET_DK].astype(BF16)
            kh = k[:, h2 * RET_DK:(h2 + 1) * RET_DK].astype(BF16)
            kdh = kd[:, h2 * RET_DK:(h2 + 1) * RET_DK].astype(BF16)
            vh = v_ref[i, :, h * RET_DV:(h + 1) * RET_DV].astype(BF16)
            s_old = s_scr[i, h]
            sc = lax.dot_general(qh, kh, _NT, preferred_element_type=F32)
            cross = jnp.dot(qh, s_old.astype(BF16), preferred_element_type=F32)
            s_add = lax.dot_general(kdh, vh, _TN, preferred_element_type=F32)
            part.append((h, vh, s_old, sc, cross, s_add))
        yield
        outs = []
        for h, vh, s_old, sc, cross, s_add in part:
            s_scr[i, h] = s_old * sdec_ref[h] + s_add
            o = jnp.dot((sc * dmask_ref[h]).astype(BF16), vh, preferred_element_type=F32)
            outs.append((h, o, cross))
        yield
        for h, o, cross in outs:
            o = o + cross * qdec_ref[h]
            o = o * lax.rsqrt(jnp.mean(o * o, axis=-1, keepdims=True) + NORM_EPS)
            gate = g_ref[i, :, h * RET_DV:(h + 1) * RET_DV]
            o_ref[i, :, h * RET_DV:(h + 1) * RET_DV] = (o * (gate * _sigmoid(gate))).astype(BF16)

    if nb == unroll:
        _round_robin(chain(i, hp) for i in range(nb) for hp in range(RET_HEADS // 2))
    else:
        def body(j, carry):
            _round_robin(chain(j * unroll + u, hp) for u in range(unroll) for hp in range(RET_HEADS // 2))
            return carry
        lax.fori_loop(0, nb // unroll, body, 0)

    @pl.when(c == pl.num_programs(1) - 1)
    def _():
        sout_ref[...] = s_scr[...]


def _ret_tables(pos, C):
    inv = 1.0 / (ROPE_BASE ** (np.arange(0, RET_DK, 2, dtype=np.float64) / RET_DK))
    ang = pos.astype(np.float64)[:, None] * inv[None, :]
    cos = np.tile(np.cos(ang), (1, 4))
    sin = np.tile(np.concatenate([-np.sin(ang), np.sin(ang)], axis=1), (1, 2))
    log_gamma = np.log(1.0 - 2.0 ** (-5.0 - np.arange(RET_HEADS, dtype=np.float64)))
    n = np.arange(C, dtype=np.float64)
    diff = n[:, None] - n[None, :]
    dmask = np.where(diff >= 0, np.exp(log_gamma[:, None, None] * np.maximum(diff, 0.0)), 0.0)
    qdec = np.broadcast_to(np.exp(log_gamma[:, None] * (n + 1.0))[:, :, None], (RET_HEADS, C, RET_DV))
    kdec = np.exp(log_gamma[:, None] * (C - 1.0 - n))
    kdec = np.repeat(kdec.reshape(RET_HEADS // 2, 2, C).transpose(0, 2, 1), RET_DK, axis=2)
    sdec = np.broadcast_to(np.exp(log_gamma * C)[:, None, None], (RET_HEADS, 1, RET_DV))
    f = lambda a: jnp.asarray(np.ascontiguousarray(a), F32)
    return f(cos), f(sin), f(dmask), f(qdec), f(kdec), f(sdec)


def _retention(p0, B, T, C, nb, unroll, tables, s0):
    cos, sin, dmask, qdec, kdec, sdec = tables
    nc = T // C
    whole = lambda a: pl.BlockSpec(a.shape, lambda b, c: (0,) * a.ndim)
    in_specs = [
        pl.BlockSpec((nb, C, RET_QK), lambda b, c: (b, c, 0)),
        pl.BlockSpec((nb, C, RET_QK), lambda b, c: (b, c, 1)),
        pl.BlockSpec((nb, C, RET_V), lambda b, c: (b, c, 2 * RET_QK // RET_V)),
        pl.BlockSpec((nb, C, RET_V), lambda b, c: (b, c, 2 * RET_QK // RET_V + 1)),
        pl.BlockSpec((C, 2 * RET_DK), lambda b, c: (c, 0)),
        pl.BlockSpec((C, 2 * RET_DK), lambda b, c: (c, 0)),
        whole(dmask), whole(qdec), whole(kdec), whole(sdec),
    ]
    args = [p0, p0, p0, p0, cos, sin, dmask, qdec, kdec, sdec]
    state_spec = pl.BlockSpec((nb, RET_HEADS, RET_DK, RET_DV), lambda b, c: (b, 0, 0, 0))
    if s0 is not None:
        in_specs.append(state_spec)
        args.append(s0)
    return pl.pallas_call(
        functools.partial(_ret_kernel, nb=nb, C=C, unroll=unroll, has_s0=s0 is not None),
        grid=(B // nb, nc),
        in_specs=in_specs,
        out_specs=[pl.BlockSpec((nb, C, RET_V), lambda b, c: (b, c, 0)), state_spec],
        out_shape=[jax.ShapeDtypeStruct((B, T, RET_V), BF16),
                   jax.ShapeDtypeStruct((B, RET_HEADS, RET_DK, RET_DV), F32)],
        scratch_shapes=[pltpu.VMEM((nb, RET_HEADS, RET_DK, RET_DV), F32)],
        compiler_params=_cparams(("parallel", "arbitrary")),
        name="retention",
    )(*args)


def _expand_blocks(y, blk, groups):
    rows, w = y.shape
    if blk == RWKV_HEAD and w == 2 * 128 and groups == 4:
        lo = lax.broadcasted_iota(jnp.int32, (rows, 128), 1) < blk
        z = jnp.zeros((rows, 128), F32)
        y0, y1 = y[:, :128], y[:, 128:]
        return jnp.concatenate([
            jnp.concatenate([jnp.where(lo, y0, 0.0), z], axis=1),
            jnp.concatenate([jnp.where(lo, 0.0, y0), z], axis=1),
            jnp.concatenate([z, jnp.where(lo, y1, 0.0)], axis=1),
            jnp.concatenate([z, jnp.where(lo, 0.0, y1)], axis=1)], axis=0)
    shift_r = rows.bit_length() - 1
    shift_l = blk.bit_length() - 1
    mask = (lax.broadcasted_iota(jnp.int32, (groups * rows, w), 0) >> shift_r) == \
           (lax.broadcasted_iota(jnp.int32, (groups * rows, w), 1) >> shift_l)
    return jnp.where(mask, jnp.concatenate([y] * groups, axis=0), 0.0)


def _rwkv_kernel(pr_ref, pk_ref, pv_ref, pg_ref, pl_ref, mur_ref, muk_ref, muv_ref, w0_ref, a0_ref, kk_ref, ka_ref,
                 rk_ref, gng_ref, gnb_ref, w2_ref, a2_ref, tri_ref, ones_ref, *rest, nb, C, unroll, has_state):
    if has_state:
        fr_ref, fk_ref, fv_ref, fl_ref, s0_ref, o_ref, sout_ref, s_scr, prev_scr = rest
    else:
        o_ref, sout_ref, s_scr, prev_scr = rest
    c = pl.program_id(2)
    G = RWKV_GROUP
    GC = G * C
    log2c = C.bit_length() - 1
    pair_w = 2 * RWKV_HEAD
    bf = lambda x: x.astype(BF16)
    dot = functools.partial(lax.dot_general, preferred_element_type=F32)

    @pl.when(c == 0)
    def _():
        s_scr[...] = jnp.zeros(s_scr.shape, F32)
        prev_scr[...] = jnp.zeros(prev_scr.shape, F32)
        if has_state:
            z = jnp.zeros((RWKV_HEAD, RWKV_HEAD), F32)
            for i in range(nb):
                for g in range(G):
                    head = s0_ref[i, g]
                    blk = jnp.concatenate([head, z] if g % 2 == 0 else [z, head], axis=1)
                    s_scr[i, g * RWKV_HEAD:(g + 1) * RWKV_HEAD, (g // 2) * pair_w:(g // 2 + 1) * pair_w] = blk

    is_row0 = lax.broadcasted_iota(jnp.int32, (C, GROUP_W), 0) == 0
    t_idx = lax.broadcasted_iota(jnp.int32, (C, GC), 0)
    s_idx = lax.broadcasted_iota(jnp.int32, (C, GC), 1) & (C - 1)
    strict = s_idx < t_idx
    incl = s_idx <= t_idx
    eye_lb = jnp.where(s_idx == t_idx, 1.0, 0.0).astype(F32)
    ones_bd = ones_ref[...]

    def seg_sum(x):
        return _dot_exact_rhs(x, ones_bd)

    def chain(i):
        def shifted(x, k, first_ref):
            prev = first_ref[pl.ds(i, 1), :] if has_state else prev_scr[k, i]
            return jnp.where(is_row0, prev, pltpu.roll(x, 1, 0))

        pr = pr_ref[i]
        pk = pk_ref[i]
        pv = pv_ref[i]
        plo = pl_ref[i]
        pr_s = shifted(pr, 0, fr_ref if has_state else None)
        pk_s = shifted(pk, 1, fk_ref if has_state else None)
        pv_s = shifted(pv, 2, fv_ref if has_state else None)
        plo_s = shifted(plo, 3, fl_ref if has_state else None)
        if not has_state:
            prev_scr[0, i] = pr[C - 1:C, :]
            prev_scr[1, i] = pk[C - 1:C, :]
            prev_scr[2, i] = pv[C - 1:C, :]
            prev_scr[3, i] = plo[C - 1:C, :]

        r = pr + mur_ref[...] * (pr_s - pr)
        kw = pk + muk_ref[...] * (pk_s - pk)
        vw = pv + muv_ref[...] * (pv_s - pv)
        pre_w = plo[:, 0:LORA] + plo_s[:, LORA:2 * LORA]
        pre_a = plo[:, 2 * LORA:3 * LORA] + plo_s[:, 3 * LORA:4 * LORA]
        wl = _dot1(jnp.tanh(pre_w), w2_ref[...])
        al = _dot1(pre_a, a2_ref[...])
        kk = kw * kk_ref[...]
        kk_ss = seg_sum(kk * kk)
        yield
        w_log = -_softplus(-(w0_ref[...] + wl)) - 0.5
        logw = -jnp.exp(w_log)
        cum = _dot_exact_rhs_lhs(tri_ref[...], logw)
        a = _sigmoid(a0_ref[...] + al)
        kk = kk / jnp.maximum(jnp.sqrt(kk_ss), 1e-12)
        k = kw * (1.0 + (a - 1.0) * ka_ref[...])
        bv = kk * a
        bonus = seg_sum(r * k * rk_ref[...])
        v_bd = bf(_expand_blocks(vw, RWKV_HEAD, G))
        yield
        tot = cum[C - 1:C, :]
        e_neg = jnp.exp(-cum)
        e_rem = jnp.exp(tot - cum)
        a_t = -kk * jnp.exp(cum - logw)
        r_t = r * jnp.exp(cum)
        b_t = bv * e_neg
        k_t = k * e_neg
        b_h = bf(bv * e_rem)
        k_h = bf(k * e_rem)

        s_old = s_scr[i]
        ar = bf(jnp.concatenate([a_t, r_t], axis=0))
        xb = dot(ar, bf(_expand_blocks(b_t, RWKV_HEAD, G)), _NT)
        xk = dot(ar, bf(_expand_blocks(k_t, RWKV_HEAD, G)), _NT)
        xs = dot(ar, bf(s_old), _NT)
        yield
        l_ab = jnp.where(strict, xb[:C], 0.0)
        l_ak = jnp.where(strict, xk[:C], 0.0)
        q = dot(bf(l_ab), bf(_expand_blocks(l_ab, C, G)), _NN)
        rhs = xs[:C] + dot(bf(l_ak), v_bd, _NN)
        p = eye_lb + l_ab
        m_rb = bf(jnp.where(incl, xb[C:], 0.0))
        m_rk = bf(jnp.where(incl, xk[C:], 0.0))
        yield
        for it in range(log2c - 1):
            q_bd = bf(_expand_blocks(q, C, G))
            if it < log2c - 2:
                pq = dot(bf(jnp.concatenate([p, q], axis=0)), q_bd, _NN)
                yield
                p = p + pq[:C]
                q = pq[C:]
            else:
                pq = dot(bf(p), q_bd, _NN)
                yield
                p = p + pq
        u = dot(bf(p), bf(_expand_blocks(rhs, RWKV_HEAD, G)), _NN)
        y_v = dot(m_rk, v_bd, _NN)
        s_v = dot(bf(vw), k_h, _TN)
        yield
        y_u = dot(m_rb, bf(_expand_blocks(u, RWKV_HEAD, G)), _NN)
        s_u = dot(bf(u), b_h, _TN)
        yield
        y = xs[C:] + y_u + y_v
        s_scr[i] = s_old * jnp.exp(tot) + _expand_blocks_mask(s_u + s_v)
        inv_n = 1.0 / RWKV_HEAD
        mean = seg_sum(y) * inv_n
        yield
        d = y - mean
        var = seg_sum(d * d) * inv_n
        yield
        yn = d * lax.rsqrt(var + GN_EPS_RWKV) * gng_ref[...] + gnb_ref[...]
        yn = yn + bonus * vw
        gate = pg_ref[i]
        o_ref[i] = (yn * (gate * _sigmoid(gate))).astype(BF16)

    if nb == unroll:
        _round_robin(chain(i) for i in range(nb))
    else:
        def body(j, carry):
            _round_robin(chain(j * unroll + u_) for u_ in range(unroll))
            return carry
        lax.fori_loop(0, nb // unroll, body, 0)

    @pl.when(c == pl.num_programs(2) - 1)
    def _():
        for i in range(nb):
            for g in range(G):
                lo = g * RWKV_HEAD
                sout_ref[i, g] = s_scr[i, lo:lo + RWKV_HEAD, lo:lo + RWKV_HEAD]


def _expand_blocks_mask(s):
    lo = lax.broadcasted_iota(jnp.int32, (RWKV_HEAD, 128), 1) < RWKV_HEAD
    z = jnp.zeros((RWKV_HEAD, 128), F32)
    out = []
    for g in range(RWKV_GROUP):
        t = s[g * RWKV_HEAD:(g + 1) * RWKV_HEAD, (g // 2) * 128:(g // 2 + 1) * 128]
        t = jnp.where(lo, t, 0.0) if g % 2 == 0 else jnp.where(lo, 0.0, t)
        out.append(jnp.concatenate([t, z] if g < 2 else [z, t], axis=1))
    return jnp.concatenate(out, axis=0)


def _dot_exact_rhs_lhs(tri, x):
    xh, xl = _split(x)
    return jnp.dot(tri, xh, preferred_element_type=F32) + jnp.dot(tri, xl, preferred_element_type=F32)


def _rwkv(p0, plora, B, T, C, nb, unroll, pfirst, pfirst_lora, s0, prm):
    nc = T // C
    ng = RWKV_HEADS // RWKV_GROUP
    gw = GROUP_W
    col = lambda off: pl.BlockSpec((nb, C, gw), lambda b, g, c: (b, c, off // gw + g))
    vec = lambda off: pl.BlockSpec((1, gw), lambda b, g, c: (0, off // gw + g))
    in_specs = [
        col(RKV_START), col(RKV_START + RWKV_W), col(RKV_START + 2 * RWKV_W), col(RKV_END),
        pl.BlockSpec((nb, C, gw), lambda b, g, c: (b, c, 0)),
        vec(0), vec(RWKV_W), vec(2 * RWKV_W),
        vec(0), vec(0), vec(0), vec(0), vec(0), vec(0), vec(0),
        pl.BlockSpec((LORA, gw), lambda b, g, c: (0, g)),
        pl.BlockSpec((LORA, gw), lambda b, g, c: (0, g)),
        pl.BlockSpec((C, C), lambda b, g, c: (0, 0)),
        pl.BlockSpec((gw, gw), lambda b, g, c: (0, 0)),
    ]
    tri = jnp.asarray(np.tril(np.ones((C, C), np.float32)), BF16)
    seg = np.arange(gw) // RWKV_HEAD
    ones_bd = jnp.asarray((seg[:, None] == seg[None, :]).astype(np.float32), BF16)
    mu = prm["mu_rkv"].reshape(1, -1)
    args = [p0, p0, p0, p0, plora, mu, mu, mu, prm["w0"], prm["a0"], prm["k_k"], prm["k_a"], prm["r_k"],
            prm["gn_g"], prm["gn_b"], prm["w2"], prm["a2"], tri, ones_bd]
    state_spec = pl.BlockSpec((nb, RWKV_GROUP, RWKV_HEAD, RWKV_HEAD), lambda b, g, c: (b, g, 0, 0))
    has_state = s0 is not None
    if has_state:
        fcol = lambda off: pl.BlockSpec((nb, gw), lambda b, g, c: (b, off // gw + g))
        in_specs += [fcol(RKV_START), fcol(RKV_START + RWKV_W), fcol(RKV_START + 2 * RWKV_W),
                     pl.BlockSpec((nb, gw), lambda b, g, c: (b, 0)), state_spec]
        args += [pfirst, pfirst, pfirst, pfirst_lora, s0]
    return pl.pallas_call(
        functools.partial(_rwkv_kernel, nb=nb, C=C, unroll=unroll, has_state=has_state),
        grid=(B // nb, ng, nc),
        in_specs=in_specs,
        out_specs=[pl.BlockSpec((nb, C, gw), lambda b, g, c: (b, c, g)), state_spec],
        out_shape=[jax.ShapeDtypeStruct((B, T, RWKV_W), BF16),
                   jax.ShapeDtypeStruct((B, RWKV_HEADS, RWKV_HEAD, RWKV_HEAD), F32)],
        scratch_shapes=[pltpu.VMEM((nb, gw, gw), F32), pltpu.VMEM((4, nb, 1, gw), F32)],
        compiler_params=_cparams(("parallel", "parallel", "arbitrary")),
        name="rwkv7",
    )(*args)


def _outproj0_kernel(x_ref, a1_ref, a2_ref, w1_ref, w2_ref, o_ref):
    acc = jnp.dot(a1_ref[...], w1_ref[...], preferred_element_type=F32)
    acc = acc + jnp.dot(a2_ref[...], w2_ref[...], preferred_element_type=F32)
    o_ref[...] = x_ref[...] + acc


def _outproj0(x, a1, a2, w, *, tm, tn):
    m, d = x.shape
    k1 = a1.shape[1]
    k2 = a2.shape[1]
    return pl.pallas_call(
        _outproj0_kernel,
        grid=(m // tm, d // tn),
        in_specs=[
            pl.BlockSpec((tm, tn), lambda i, j: (i, j)),
            pl.BlockSpec((tm, k1), lambda i, j: (i, 0)),
            pl.BlockSpec((tm, k2), lambda i, j: (i, 0)),
            pl.BlockSpec((k1, tn), lambda i, j: (0, j)),
            pl.BlockSpec((k2, tn), lambda i, j: (k1 // k2, j)),
        ],
        out_specs=pl.BlockSpec((tm, tn), lambda i, j: (i, j)),
        out_shape=jax.ShapeDtypeStruct((m, d), F32),
        compiler_params=_cparams(("parallel", "arbitrary")),
        name="outproj0",
    )(x, a1, a2, w, w)


def _outproj1_kernel(x_ref, y_ref, w_ref, g_ref, o_ref):
    out = x_ref[...] + jnp.dot(y_ref[...], w_ref[...], preferred_element_type=F32)
    o_ref[...] = out * lax.rsqrt(jnp.mean(out * out, axis=-1, keepdims=True) + NORM_EPS) * g_ref[...]


def _outproj1_norm(x, y, w, g, *, tm):
    m, d = x.shape
    k = y.shape[1]
    return pl.pallas_call(
        _outproj1_kernel,
        grid=(m // tm,),
        in_specs=[
            pl.BlockSpec((tm, d), lambda i: (i, 0)),
            pl.BlockSpec((tm, k), lambda i: (i, 0)),
            pl.BlockSpec((k, d), lambda i: (0, 0)),
            pl.BlockSpec((1, d), lambda i: (0, 0)),
        ],
        out_specs=pl.BlockSpec((tm, d), lambda i: (i, 0)),
        out_shape=jax.ShapeDtypeStruct((m, d), F32),
        compiler_params=_cparams(("parallel",)),
        name="outproj1_norm",
    )(x, y, w, g.reshape(1, d))


def _lru_gates(xc, wa_ref, wx_ref, ba, bx, spl):
    xcb = xc.astype(BF16)
    ra, rx = [], []
    for n in range(LRU_BLOCKS):
        blk = xcb[:, n * LRU_BW:(n + 1) * LRU_BW]
        ra.append(jnp.dot(blk, wa_ref[n], preferred_element_type=F32))
        rx.append(jnp.dot(blk, wx_ref[n], preferred_element_type=F32))
    gate_r = _sigmoid(jnp.concatenate(ra, axis=1) + ba)
    gate_i = _sigmoid(jnp.concatenate(rx, axis=1) + bx)
    log_a = -LRU_C * gate_r * spl
    a = jnp.exp(log_a)
    mult = jnp.sqrt(-jnp.tanh(log_a) * (a * a + 1.0))
    return a, mult, gate_i * xc


def _lru_prompt_kernel(xb_ref, gb_ref, cw_ref, cb_ref, wa_ref, wx_ref, ba_ref, bx_ref, lam_ref,
                       y_ref, hl_ref, xp_scr, a_scr, b_scr, h_scr, *, tt):
    i = pl.program_id(1)
    pad = 8

    @pl.when(i == 0)
    def _():
        xp_scr[0:pad, :] = jnp.zeros((pad, LRU_W), F32)
        h_scr[...] = jnp.zeros(h_scr.shape, F32)

    xp_scr[pad:pad + tt, :] = xb_ref[...]
    xc = cb_ref[...] + xp_scr[pad - 3:pad - 3 + tt, :] * cw_ref[0:1, :]
    for j in range(1, CONV_W):
        xc = xc + xp_scr[pad - 3 + j:pad - 3 + j + tt, :] * cw_ref[j:j + 1, :]
    xp_scr[0:pad, :] = xp_scr[tt:tt + pad, :]

    spl = _softplus(-lam_ref[...])
    a, mult, gx = _lru_gates(xc, wa_ref, wx_ref, ba_ref[...], bx_ref[...], spl)
    row = lax.broadcasted_iota(jnp.int32, (tt, LRU_W), 0)
    mult = jnp.where((row == 0) & (i == 0), 1.0, mult)
    a_scr[...] = a
    b_scr[...] = mult * gx

    def body(s, h):
        r0 = pl.multiple_of(s * 8, 8)
        a8 = a_scr[pl.ds(r0, 8), :]
        b8 = b_scr[pl.ds(r0, 8), :]
        out = []
        for j in range(8):
            h = a8[j:j + 1, :] * h + b8[j:j + 1, :]
            out.append(h)
        b_scr[pl.ds(r0, 8), :] = jnp.concatenate(out, axis=0)
        return h

    h = lax.fori_loop(0, tt // 8, body, h_scr[...])
    h_scr[...] = h
    gb = gb_ref[...]
    y_ref[...] = (b_scr[...] * (gb * _sigmoid(gb))).astype(BF16)

    @pl.when(i == pl.num_programs(1) - 1)
    def _():
        hl_ref[0] = h


def _lru_prompt(p1, B, T, prm, *, tt):
    nt = T // tt
    vec = pl.BlockSpec((1, LRU_W), lambda b, i: (0, 0))
    wspec = pl.BlockSpec((LRU_BLOCKS, LRU_BW, LRU_BW), lambda b, i: (0, 0, 0))
    return pl.pallas_call(
        functools.partial(_lru_prompt_kernel, tt=tt),
        grid=(B, nt),
        in_specs=[
            pl.BlockSpec((tt, LRU_W), lambda b, i: (b * nt + i, 0)),
            pl.BlockSpec((tt, LRU_W), lambda b, i: (b * nt + i, 1)),
            pl.BlockSpec((CONV_W, LRU_W), lambda b, i: (0, 0)),
            vec, wspec, wspec, vec, vec, vec,
        ],
        out_specs=[pl.BlockSpec((tt, LRU_W), lambda b, i: (b * nt + i, 0)),
                   pl.BlockSpec((1, 1, LRU_W), lambda b, i: (b, 0, 0))],
        out_shape=[jax.ShapeDtypeStruct((B * T, LRU_W), BF16), jax.ShapeDtypeStruct((B, 1, LRU_W), F32)],
        scratch_shapes=[pltpu.VMEM((tt + 8, LRU_W), F32), pltpu.VMEM((tt, LRU_W), F32),
                        pltpu.VMEM((tt, LRU_W), F32), pltpu.VMEM((1, LRU_W), F32)],
        compiler_params=_cparams(("parallel", "arbitrary")),
        name="lru_prompt",
    )(p1, p1, prm["conv_w"], prm["conv_b"], prm["wa"], prm["wx"], prm["ba"], prm["bx"], prm["lam"])


def _lru_sample_kernel(xb_ref, gb_ref, cbuf_ref, h0_ref, cw_ref, cb_ref, wa_ref, wx_ref, ba_ref, bx_ref, lam_ref,
                       y_ref, hl_ref, *, T):
    spl = _softplus(-lam_ref[...])
    h = h0_ref[...]
    for t in range(T):
        xc = cb_ref[...]
        for j in range(CONV_W):
            src = t + j - (CONV_W - 1)
            tap = xb_ref[src] if src >= 0 else cbuf_ref[src + CONV_W - 1]
            xc = xc + tap * cw_ref[j:j + 1, :]
        a, mult, gx = _lru_gates(xc, wa_ref, wx_ref, ba_ref[...], bx_ref[...], spl)
        h = a * h + mult * gx
        gb = gb_ref[t]
        y_ref[t] = (h * (gb * _sigmoid(gb))).astype(BF16)
    hl_ref[...] = h


def _lru_sample(p1_tm, cbuf_tm, h0, prm, *, bt):
    T, B, _ = p1_tm.shape
    vec = pl.BlockSpec((1, LRU_W), lambda i: (0, 0))
    wspec = pl.BlockSpec((LRU_BLOCKS, LRU_BW, LRU_BW), lambda i: (0, 0, 0))
    return pl.pallas_call(
        functools.partial(_lru_sample_kernel, T=T),
        grid=(B // bt,),
        in_specs=[
            pl.BlockSpec((T, bt, LRU_W), lambda i: (0, i, 0)),
            pl.BlockSpec((T, bt, LRU_W), lambda i: (0, i, 1)),
            pl.BlockSpec((CONV_W - 1, bt, LRU_W), lambda i: (0, i, 0)),
            pl.BlockSpec((bt, LRU_W), lambda i: (i, 0)),
            pl.BlockSpec((CONV_W, LRU_W), lambda i: (0, 0)),
            vec, wspec, wspec, vec, vec, vec,
        ],
        out_specs=[pl.BlockSpec((T, bt, LRU_W), lambda i: (0, i, 0)), pl.BlockSpec((bt, LRU_W), lambda i: (i, 0))],
        out_shape=[jax.ShapeDtypeStruct((T, B, LRU_W), BF16), jax.ShapeDtypeStruct((B, LRU_W), F32)],
        compiler_params=_cparams(("parallel",)),
        name="lru_sample",
    )(p1_tm, p1_tm, cbuf_tm, h0, prm["conv_w"], prm["conv_b"], prm["wa"], prm["wx"], prm["ba"], prm["bx"], prm["lam"])


def kernel(x_prompt, x_sample, state_ret, state_rwkv_wkv, state_rwkv_shift, state_lru_h, state_lru_conv,
           norm0_g, w_in0, w_out0, rwkv_mu_rkv, rwkv_mu_w, rwkv_mu_a, rwkv_w0, rwkv_w1, rwkv_w2,
           rwkv_a0, rwkv_a1, rwkv_a2, rwkv_k_k, rwkv_k_a, rwkv_r_k, rwkv_gn_g, rwkv_gn_b,
           norm1_g, w_in1, lru_conv_w, lru_conv_b, lru_wa, lru_ba, lru_wx, lru_bx, lru_lambda, w_out1,
           final_g):
    bp, tp, d = x_prompt.shape
    bs, ts, _ = x_sample.shape
    xp = x_prompt.reshape(bp * tp, d)
    xs = x_sample.reshape(bs * ts, d)

    mw = rwkv_mu_w[:, None]
    ma = rwkv_mu_a[:, None]
    w_lora = jnp.concatenate([(1.0 - mw) * rwkv_w1, mw * rwkv_w1, (1.0 - ma) * rwkv_a1, ma * rwkv_a1],
                             axis=1).astype(BF16)
    w_in0_b = w_in0.astype(BF16)
    w_out0_b = w_out0.astype(BF16)
    w_in1_b = w_in1.astype(BF16)
    w_out1_b = w_out1.astype(BF16)
    row = lambda v: v.reshape(1, -1)
    rwkv_prm = dict(mu_rkv=rwkv_mu_rkv, w0=row(rwkv_w0), a0=row(rwkv_a0), k_k=row(rwkv_k_k), k_a=row(rwkv_k_a),
                    r_k=row(rwkv_r_k), gn_g=row(rwkv_gn_g), gn_b=row(rwkv_gn_b),
                    w2=rwkv_w2.astype(BF16), a2=rwkv_a2.astype(BF16))
    lru_prm = dict(conv_w=lru_conv_w, conv_b=row(lru_conv_b), wa=lru_wa.astype(BF16), wx=lru_wx.astype(BF16),
                   ba=row(lru_ba), bx=row(lru_bx), lam=row(lru_lambda))

    p0p, plp = _norm_matmul(xp, norm0_g, w_in0_b, w_lora, apply_norm=True, tm=_tile(bp * tp, 1024), tn=512)
    p0s, pls = _norm_matmul(xs, norm0_g, w_in0_b, w_lora, apply_norm=True, tm=bs * ts, tn=512)
    pfirst, pfirst_lora = _norm_matmul(state_rwkv_shift, norm0_g, w_in0_b, w_lora, apply_norm=False, tm=bs, tn=512)
    p0p = p0p.reshape(bp, tp, IN0)
    p0s = p0s.reshape(bs, ts, IN0)

    cp = RET_CHUNK if tp % RET_CHUNK == 0 else tp
    cs = RET_CHUNK if ts % RET_CHUNK == 0 else ts
    ret_p, new_ret_p = _retention(p0p, bp, tp, cp, 1, 1, _ret_tables(np.arange(tp), cp), None)
    ret_s, new_ret_s = _retention(p0s, bs, ts, cs, 8, 2, _ret_tables(PAST_LEN + np.arange(ts), cs), state_ret)

    cwp = RWKV_CHUNK if tp % RWKV_CHUNK == 0 else tp
    rw_p, wkv_p = _rwkv(p0p, plp.reshape(bp, tp, 4 * LORA), bp, tp, cwp, bp, bp, None, None, None, rwkv_prm)
    rw_s, wkv_s = _rwkv(p0s, pls.reshape(bs, ts, 4 * LORA), bs, ts, ts, 8, 4, pfirst, pfirst_lora, state_rwkv_wkv,
                        rwkv_prm)
    ret_p = ret_p.reshape(bp * tp, RET_V)
    ret_s = ret_s.reshape(bs * ts, RET_V)
    rw_p = rw_p.reshape(bp * tp, RWKV_W)
    rw_s = rw_s.reshape(bs * ts, RWKV_W)

    x1p = _outproj0(xp, ret_p, rw_p, w_out0_b, tm=512, tn=1024)
    x1s = _outproj0(xs, ret_s, rw_s, w_out0_b, tm=512, tn=1024)

    last_rows = jnp.concatenate([x_prompt[:, -1, :], x_sample[:, -1, :]], axis=0)
    shift = _rmsnorm_rows(last_rows, norm0_g)

    p1p = _norm_matmul(x1p, norm1_g, w_in1_b, apply_norm=True, tm=_tile(bp * tp, 1024), tn=512)
    x1s_tm = x1s.reshape(bs, ts, d).transpose(1, 0, 2).reshape(ts * bs, d)
    p1s = _norm_matmul(x1s_tm, norm1_g, w_in1_b, apply_norm=True, tm=512, tn=512)

    yp, lh_p = _lru_prompt(p1p, bp, tp, lru_prm, tt=256)
    ys_tm, lh_s = _lru_sample(p1s.reshape(ts, bs, 2 * LRU_W), state_lru_conv.transpose(1, 0, 2), state_lru_h,
                              lru_prm, bt=64)

    out_p = _outproj1_norm(x1p, yp, w_out1_b, final_g, tm=256)
    out_s_tm = _outproj1_norm(x1s_tm, ys_tm.reshape(ts * bs, LRU_W), w_out1_b, final_g, tm=256)

    y_prompt = out_p.reshape(bp, tp, d)
    y_sample = out_s_tm.reshape(ts, bs, d).transpose(1, 0, 2)
    lc_p = p1p.reshape(bp, tp, 2 * LRU_W)[:, tp - (CONV_W - 1):, :LRU_W]
    lc_s = p1s.reshape(ts, bs, 2 * LRU_W)[ts - (CONV_W - 1):, :, :LRU_W].transpose(1, 0, 2)
    return (y_prompt, y_sample,
            new_ret_p, wkv_p, shift[:bp], lh_p.reshape(bp, LRU_W), lc_p,
            new_ret_s, wkv_s, shift[bp:], lh_s, lc_s)
```

```python
import functools

import numpy as np
import jax
import jax.numpy as jnp
from jax import lax
from jax.experimental import pallas as pl
from jax.experimental.pallas import tpu as pltpu

F32 = jnp.float32
BF16 = jnp.bfloat16

D_MODEL = 2048
PAST_LEN = 16384
RET_HEADS = 8
RET_DK = 64
RET_DV = 128
RET_QK = RET_HEADS * RET_DK
RET_V = RET_HEADS * RET_DV
RET_CHUNK = 128
ROPE_BASE = 10000.0
RWKV_HEAD = 64
RWKV_HEADS = 16
RWKV_W = RWKV_HEADS * RWKV_HEAD
LORA = 64
GN_EPS_RWKV = 64e-5
LRU_W = 2048
LRU_BLOCKS = 8
LRU_BW = LRU_W // LRU_BLOCKS
CONV_W = 4
LRU_C = 8.0
NORM_EPS = 1e-6
RKV_START = 2 * RET_QK + 2 * RET_V
RKV_END = RKV_START + 3 * RWKV_W
IN0 = RKV_END + RWKV_W

RWKV_GROUP = 4
GROUP_W = RWKV_GROUP * RWKV_HEAD
RWKV_CHUNK = 64

VMEM_LIMIT = 48 * 1024 * 1024


def _tile(m, pref):
    return pref if m % pref == 0 else m


def _round_robin(gens):
    gens = list(gens)
    while gens:
        alive = []
        for g in gens:
            try:
                next(g)
                alive.append(g)
            except StopIteration:
                pass
        gens = alive


def _cparams(sem):
    return pltpu.CompilerParams(dimension_semantics=sem, vmem_limit_bytes=VMEM_LIMIT)


def _split(x):
    hi = x.astype(BF16)
    lo = (x - hi.astype(F32)).astype(BF16)
    return hi, lo


_NN = (((1,), (0,)), ((), ()))
_NT = (((1,), (1,)), ((), ()))
_TN = (((0,), (0,)), ((), ()))


def _dot1(a, b, dims=_NN):
    return lax.dot_general(a.astype(BF16), b.astype(BF16), dims, preferred_element_type=F32)


def _dot_exact_rhs(a, b, dims=_NN):
    ah, al = _split(a)
    d = functools.partial(lax.dot_general, dimension_numbers=dims, preferred_element_type=F32)
    return d(ah, b) + d(al, b)


def _softplus(x):
    return jnp.maximum(x, 0.0) + jnp.log1p(jnp.exp(-jnp.abs(x)))


def _sigmoid(x):
    return 1.0 / (1.0 + jnp.exp(-x))


def _norm_matmul_kernel(x_ref, g_ref, w_ref, *rest, apply_norm, has_extra):
    if has_extra:
        we_ref, o_ref, oe_ref, h_ref = rest
    else:
        o_ref, h_ref = rest

    @pl.when(pl.program_id(1) == 0)
    def _():
        x = x_ref[...]
        if apply_norm:
            x = x * lax.rsqrt(jnp.mean(x * x, axis=-1, keepdims=True) + NORM_EPS) * g_ref[...]
        h_ref[...] = x.astype(BF16)
        if has_extra:
            oe_ref[...] = jnp.dot(h_ref[...], we_ref[...], preferred_element_type=F32)

    o_ref[...] = jnp.dot(h_ref[...], w_ref[...], preferred_element_type=F32)


def _norm_matmul(x, g, w, w_extra=None, *, apply_norm, tm, tn):
    m, d = x.shape
    n = w.shape[1]
    has_extra = w_extra is not None
    in_specs = [
        pl.BlockSpec((tm, d), lambda i, j: (i, 0)),
        pl.BlockSpec((1, d), lambda i, j: (0, 0)),
        pl.BlockSpec((d, tn), lambda i, j: (0, j)),
    ]
    out_specs = [pl.BlockSpec((tm, tn), lambda i, j: (i, j))]
    out_shape = [jax.ShapeDtypeStruct((m, n), F32)]
    args = [x, g.reshape(1, d), w]
    if has_extra:
        ne = w_extra.shape[1]
        in_specs.append(pl.BlockSpec((d, ne), lambda i, j: (0, 0)))
        out_specs.append(pl.BlockSpec((tm, ne), lambda i, j: (i, 0)))
        out_shape.append(jax.ShapeDtypeStruct((m, ne), F32))
        args.append(w_extra)
    out = pl.pallas_call(
        functools.partial(_norm_matmul_kernel, apply_norm=apply_norm, has_extra=has_extra),
        grid=(m // tm, n // tn),
        in_specs=in_specs,
        out_specs=out_specs,
        out_shape=out_shape,
        scratch_shapes=[pltpu.VMEM((tm, d), BF16)],
        compiler_params=_cparams(("parallel", "arbitrary")),
        name="norm_matmul",
    )(*args)
    return out if has_extra else out[0]


def _rmsnorm_kernel(x_ref, g_ref, o_ref):
    x = x_ref[...]
    o_ref[...] = x * lax.rsqrt(jnp.mean(x * x, axis=-1, keepdims=True) + NORM_EPS) * g_ref[...]


def _rmsnorm_rows(x, g):
    return pl.pallas_call(
        _rmsnorm_kernel,
        out_shape=jax.ShapeDtypeStruct(x.shape, F32),
        name="rmsnorm_rows",
    )(x, g.reshape(1, -1))


def _ret_kernel(q_ref, k_ref, v_ref, g_ref, cos_ref, sin_ref, dmask_ref, qdec_ref, kdec_ref, sdec_ref,
                *rest, nb, C, unroll, has_s0):
    if has_s0:
        s0_ref, o_ref, sout_ref, s_scr = rest
    else:
        o_ref, sout_ref, s_scr = rest
    c = pl.program_id(1)

    @pl.when(c == 0)
    def _():
        if has_s0:
            s_scr[...] = s0_ref[...]
        else:
            s_scr[...] = jnp.zeros(s_scr.shape, F32)

    pw = 2 * RET_DK
    lane = lax.broadcasted_iota(jnp.int32, (C, pw), 1)
    first_half = (lane & (RET_DK // 2)) == 0
    cos = cos_ref[...]
    sin = sin_ref[...]

    def rope(x):
        rot = jnp.where(first_half, pltpu.roll(x, pw - RET_DK // 2, 1), pltpu.roll(x, RET_DK // 2, 1))
        return x * cos + rot * sin

    def chain(i, hp):
        q = rope(q_ref[i, :, hp * pw:(hp + 1) * pw])
        k = rope(k_ref[i, :, hp * pw:(hp + 1) * pw]) * (RET_DK ** -0.5)
        kd = k * kdec_ref[hp]
        part = []
        for h2 in range(2):
            h = 2 * hp + h2
            qh = q[:, h2 * RET_DK:(h2 + 1) * RET_DK].astype(BF16)
            kh = k[:, h2 * RET_DK:(h2 + 1) * RET_DK].astype(BF16)
            kdh = kd[:, h2 * RET_DK:(h2 + 1) * RET_DK].astype(BF16)
            vh = v_ref[i, :, h * RET_DV:(h + 1) * RET_DV].astype(BF16)
            s_old = s_scr[i, h]
            sc = lax.dot_general(qh, kh, _NT, preferred_element_type=F32)
            cross = jnp.dot(qh, s_old.astype(BF16), preferred_element_type=F32)
            s_add = lax.dot_general(kdh, vh, _TN, preferred_element_type=F32)
            part.append((h, vh, s_old, sc, cross, s_add))
        yield
        outs = []
        for h, vh, s_old, sc, cross, s_add in part:
            s_scr[i, h] = s_old * sdec_ref[h] + s_add
            o = jnp.dot((sc * dmask_ref[h]).astype(BF16), vh, preferred_element_type=F32)
            outs.append((h, o, cross))
        yield
        for h, o, cross in outs:
            o = o + cross * qdec_ref[h]
            o = o * lax.rsqrt(jnp.mean(o * o, axis=-1, keepdims=True) + NORM_EPS)
            gate = g_ref[i, :, h * RET_DV:(h + 1) * RET_DV]
            o_ref[i, :, h * RET_DV:(h + 1) * RET_DV] = (o * (gate * _sigmoid(gate))).astype(BF16)

    if nb == unroll:
        _round_robin(chain(i, hp) for i in range(nb) for hp in range(RET_HEADS // 2))
    else:
        def body(j, carry):
            _round_robin(chain(j * unroll + u, hp) for u in range(unroll) for hp in range(RET_HEADS // 2))
            return carry
        lax.fori_loop(0, nb // unroll, body, 0)

    @pl.when(c == pl.num_programs(1) - 1)
    def _():
        sout_ref[...] = s_scr[...]


def _ret_tables(pos, C):
    inv = 1.0 / (ROPE_BASE ** (np.arange(0, RET_DK, 2, dtype=np.float64) / RET_DK))
    ang = pos.astype(np.float64)[:, None] * inv[None, :]
    cos = np.tile(np.cos(ang), (1, 4))
    sin = np.tile(np.concatenate([-np.sin(ang), np.sin(ang)], axis=1), (1, 2))
    log_gamma = np.log(1.0 - 2.0 ** (-5.0 - np.arange(RET_HEADS, dtype=np.float64)))
    n = np.arange(C, dtype=np.float64)
    diff = n[:, None] - n[None, :]
    dmask = np.where(diff >= 0, np.exp(log_gamma[:, None, None] * np.maximum(diff, 0.0)), 0.0)
    qdec = np.broadcast_to(np.exp(log_gamma[:, None] * (n + 1.0))[:, :, None], (RET_HEADS, C, RET_DV))
    kdec = np.exp(log_gamma[:, None] * (C - 1.0 - n))
    kdec = np.repeat(kdec.reshape(RET_HEADS // 2, 2, C).transpose(0, 2, 1), RET_DK, axis=2)
    sdec = np.broadcast_to(np.exp(log_gamma * C)[:, None, None], (RET_HEADS, 1, RET_DV))
    f = lambda a: jnp.asarray(np.ascontiguousarray(a), F32)
    return f(cos), f(sin), f(dmask), f(qdec), f(kdec), f(sdec)


def _retention(p0, B, T, C, nb, unroll, tables, s0):
    cos, sin, dmask, qdec, kdec, sdec = tables
    nc = T // C
    whole = lambda a: pl.BlockSpec(a.shape, lambda b, c: (0,) * a.ndim)
    in_specs = [
        pl.BlockSpec((nb, C, RET_QK), lambda b, c: (b, c, 0)),
        pl.BlockSpec((nb, C, RET_QK), lambda b, c: (b, c, 1)),
        pl.BlockSpec((nb, C, RET_V), lambda b, c: (b, c, 2 * RET_QK // RET_V)),
        pl.BlockSpec((nb, C, RET_V), lambda b, c: (b, c, 2 * RET_QK // RET_V + 1)),
        pl.BlockSpec((C, 2 * RET_DK), lambda b, c: (c, 0)),
        pl.BlockSpec((C, 2 * RET_DK), lambda b, c: (c, 0)),
        whole(dmask), whole(qdec), whole(kdec), whole(sdec),
    ]
    args = [p0, p0, p0, p0, cos, sin, dmask, qdec, kdec, sdec]
    state_spec = pl.BlockSpec((nb, RET_HEADS, RET_DK, RET_DV), lambda b, c: (b, 0, 0, 0))
    if s0 is not None:
        in_specs.append(state_spec)
        args.append(s0)
    return pl.pallas_call(
        functools.partial(_ret_kernel, nb=nb, C=C, unroll=unroll, has_s0=s0 is not None),
        grid=(B // nb, nc),
        in_specs=in_specs,
        out_specs=[pl.BlockSpec((nb, C, RET_V), lambda b, c: (b, c, 0)), state_spec],
        out_shape=[jax.ShapeDtypeStruct((B, T, RET_V), BF16),
                   jax.ShapeDtypeStruct((B, RET_HEADS, RET_DK, RET_DV), F32)],
        scratch_shapes=[pltpu.VMEM((nb, RET_HEADS, RET_DK, RET_DV), F32)],
        compiler_params=_cparams(("parallel", "arbitrary")),
        name="retention",
    )(*args)


def _expand_blocks(y, blk, groups):
    rows, w = y.shape
    if blk == RWKV_HEAD and w == 2 * 128 and groups == 4:
        lo = lax.broadcasted_iota(jnp.int32, (rows, 128), 1) < blk
        z = jnp.zeros((rows, 128), F32)
        y0, y1 = y[:, :128], y[:, 128:]
        return jnp.concatenate([
            jnp.concatenate([jnp.where(lo, y0, 0.0), z], axis=1),
            jnp.concatenate([jnp.where(lo, 0.0, y0), z], axis=1),
            jnp.concatenate([z, jnp.where(lo, y1, 0.0)], axis=1),
            jnp.concatenate([z, jnp.where(lo, 0.0, y1)], axis=1)], axis=0)
    shift_r = rows.bit_length() - 1
    shift_l = blk.bit_length() - 1
    mask = (lax.broadcasted_iota(jnp.int32, (groups * rows, w), 0) >> shift_r) == \
           (lax.broadcasted_iota(jnp.int32, (groups * rows, w), 1) >> shift_l)
    return jnp.where(mask, jnp.concatenate([y] * groups, axis=0), 0.0)


def _rwkv_kernel(pr_ref, pk_ref, pv_ref, pg_ref, pl_ref, mur_ref, muk_ref, muv_ref, w0_ref, a0_ref, kk_ref, ka_ref,
                 rk_ref, gng_ref, gnb_ref, w2_ref, a2_ref, tri_ref, ones_ref, *rest, nb, C, gpb, has_state):
    if has_state:
        fr_ref, fk_ref, fv_ref, fl_ref, s0_ref, o_ref, sout_ref, s_scr, prev_scr, prevl_scr = rest
    else:
        o_ref, sout_ref, s_scr, prev_scr, prevl_scr = rest
    c = pl.program_id(2)
    G = RWKV_GROUP
    GC = G * C
    log2c = C.bit_length() - 1
    pair_w = 2 * RWKV_HEAD
    bf = lambda x: x.astype(BF16)
    dot = functools.partial(lax.dot_general, preferred_element_type=F32)

    @pl.when(c == 0)
    def _():
        s_scr[...] = jnp.zeros(s_scr.shape, F32)
        prev_scr[...] = jnp.zeros(prev_scr.shape, F32)
        prevl_scr[...] = jnp.zeros(prevl_scr.shape, F32)
        if has_state:
            z = jnp.zeros((RWKV_HEAD, RWKV_HEAD), F32)
            for i in range(nb):
                for hg in range(gpb * G):
                    g = hg % G
                    head = s0_ref[i, hg]
                    blk = jnp.concatenate([head, z] if g % 2 == 0 else [z, head], axis=1)
                    s_scr[i * gpb + hg // G, g * RWKV_HEAD:(g + 1) * RWKV_HEAD,
                          (g // 2) * pair_w:(g // 2 + 1) * pair_w] = blk

    is_row0 = lax.broadcasted_iota(jnp.int32, (C, GROUP_W), 0) == 0
    t_idx = lax.broadcasted_iota(jnp.int32, (C, GC), 0)
    s_idx = lax.broadcasted_iota(jnp.int32, (C, GC), 1) & (C - 1)
    strict = s_idx < t_idx
    incl = s_idx <= t_idx
    eye_lb = jnp.where(s_idx == t_idx, 1.0, 0.0).astype(F32)
    ones_bd = ones_ref[...]

    def seg_sum(x):
        return _dot1(x, ones_bd)

    def lora_inputs(i):
        plo = pl_ref[i]
        prev = fl_ref[pl.ds(i, 1), :] if has_state else prevl_scr[i]
        plo_s = jnp.where(is_row0, prev, pltpu.roll(plo, 1, 0))
        if not has_state:
            prevl_scr[i] = plo[C - 1:C, :]
        pre_w = plo[:, 0:LORA] + plo_s[:, LORA:2 * LORA]
        pre_a = plo[:, 2 * LORA:3 * LORA] + plo_s[:, 3 * LORA:4 * LORA]
        return bf(jnp.tanh(pre_w)), bf(pre_a)

    lora = [lora_inputs(i) for i in range(nb)]

    def chain(i, gg):
        cols = slice(gg * GROUP_W, (gg + 1) * GROUP_W)

        def shifted(x, k, first_ref):
            prev = first_ref[pl.ds(i, 1), cols] if has_state else prev_scr[k, i, :, cols]
            return jnp.where(is_row0, prev, pltpu.roll(x, 1, 0))

        pr = pr_ref[i, :, cols]
        pk = pk_ref[i, :, cols]
        pv = pv_ref[i, :, cols]
        pr_s = shifted(pr, 0, fr_ref if has_state else None)
        pk_s = shifted(pk, 1, fk_ref if has_state else None)
        pv_s = shifted(pv, 2, fv_ref if has_state else None)
        if not has_state:
            prev_scr[0, i, :, cols] = pr[C - 1:C, :]
            prev_scr[1, i, :, cols] = pk[C - 1:C, :]
            prev_scr[2, i, :, cols] = pv[C - 1:C, :]

        r = pr + mur_ref[:, cols] * (pr_s - pr)
        kw = pk + muk_ref[:, cols] * (pk_s - pk)
        vw = pv + muv_ref[:, cols] * (pv_s - pv)
        wl = dot(lora[i][0], w2_ref[:, cols], _NN)
        al = dot(lora[i][1], a2_ref[:, cols], _NN)
        kk = kw * kk_ref[:, cols]
        kk_ss = seg_sum(kk * kk)
        yield
        w_log = -_softplus(-(w0_ref[:, cols] + wl)) - 0.5
        logw = -jnp.exp(w_log)
        cum = _dot_exact_rhs_lhs(tri_ref[...], logw)
        a = _sigmoid(a0_ref[:, cols] + al)
        kk = kk / jnp.maximum(jnp.sqrt(kk_ss), 1e-12)
        k = kw * (1.0 + (a - 1.0) * ka_ref[:, cols])
        bv = kk * a
        bonus = seg_sum(r * k * rk_ref[:, cols])
        v_bd = bf(_expand_blocks(vw, RWKV_HEAD, G))
        yield
        tot = cum[C - 1:C, :]
        e_neg = jnp.exp(-cum)
        e_rem = jnp.exp(tot - cum)
        a_t = -kk * jnp.exp(cum - logw)
        r_t = r * jnp.exp(cum)
        b_t = bv * e_neg
        k_t = k * e_neg
        b_h = bf(bv * e_rem)
        k_h = bf(k * e_rem)

        s_old = s_scr[i * gpb + gg]
        ar = bf(jnp.concatenate([a_t, r_t], axis=0))
        xb = dot(ar, bf(_expand_blocks(b_t, RWKV_HEAD, G)), _NT)
        xk = dot(ar, bf(_expand_blocks(k_t, RWKV_HEAD, G)), _NT)
        xs = dot(ar, bf(s_old), _NT)
        yield
        l_ab = jnp.where(strict, xb[:C], 0.0)
        l_ak = jnp.where(strict, xk[:C], 0.0)
        q = dot(bf(l_ab), bf(_expand_blocks(l_ab, C, G)), _NN)
        rhs = xs[:C] + dot(bf(l_ak), v_bd, _NN)
        p = eye_lb + l_ab
        m_rb = bf(jnp.where(incl, xb[C:], 0.0))
        m_rk = bf(jnp.where(incl, xk[C:], 0.0))
        yield
        for it in range(log2c - 1):
            q_bd = bf(_expand_blocks(q, C, G))
            if it < log2c - 2:
                pq = dot(bf(jnp.concatenate([p, q], axis=0)), q_bd, _NN)
                yield
                p = p + pq[:C]
                q = pq[C:]
            else:
                pq = dot(bf(p), q_bd, _NN)
                yield
                p = p + pq
        u = dot(bf(p), bf(_expand_blocks(rhs, RWKV_HEAD, G)), _NN)
        y_v = dot(m_rk, v_bd, _NN)
        s_v = dot(bf(vw), k_h, _TN)
        yield
        y_u = dot(m_rb, bf(_expand_blocks(u, RWKV_HEAD, G)), _NN)
        s_u = dot(bf(u), b_h, _TN)
        yield
        y = xs[C:] + y_u + y_v
        s_scr[i * gpb + gg] = s_old * jnp.exp(tot) + _expand_blocks_mask(s_u + s_v)
        inv_n = 1.0 / RWKV_HEAD
        mean = seg_sum(y) * inv_n
        yield
        d = y - mean
        var = seg_sum(d * d) * inv_n
        yield
        yn = d * lax.rsqrt(var + GN_EPS_RWKV) * gng_ref[:, cols] + gnb_ref[:, cols]
        yn = yn + bonus * vw
        gate = pg_ref[i, :, cols]
        o_ref[i, :, cols] = (yn * (gate * _sigmoid(gate))).astype(BF16)

    _round_robin(chain(i, gg) for i in range(nb) for gg in range(gpb))

    @pl.when(c == pl.num_programs(2) - 1)
    def _():
        for i in range(nb):
            for hg in range(gpb * G):
                lo = (hg % G) * RWKV_HEAD
                sout_ref[i, hg] = s_scr[i * gpb + hg // G, lo:lo + RWKV_HEAD, lo:lo + RWKV_HEAD]


def _expand_blocks_mask(s):
    lo = lax.broadcasted_iota(jnp.int32, (RWKV_HEAD, 128), 1) < RWKV_HEAD
    z = jnp.zeros((RWKV_HEAD, 128), F32)
    out = []
    for g in range(RWKV_GROUP):
        t = s[g * RWKV_HEAD:(g + 1) * RWKV_HEAD, (g // 2) * 128:(g // 2 + 1) * 128]
        t = jnp.where(lo, t, 0.0) if g % 2 == 0 else jnp.where(lo, 0.0, t)
        out.append(jnp.concatenate([t, z] if g < 2 else [z, t], axis=1))
    return jnp.concatenate(out, axis=0)


def _dot_exact_rhs_lhs(tri, x):
    xh, xl = _split(x)
    return jnp.dot(tri, xh, preferred_element_type=F32) + jnp.dot(tri, xl, preferred_element_type=F32)


def _rwkv(p0, plora, B, T, C, nb, gpb, pfirst, pfirst_lora, s0, prm):
    nc = T // C
    ng = RWKV_HEADS // (RWKV_GROUP * gpb)
    gw = GROUP_W
    bw = gpb * gw
    col = lambda off: pl.BlockSpec((nb, C, bw), lambda b, g, c: (b, c, off // bw + g))
    vec = lambda off: pl.BlockSpec((1, bw), lambda b, g, c: (0, off // bw + g))
    in_specs = [
        col(RKV_START), col(RKV_START + RWKV_W), col(RKV_START + 2 * RWKV_W), col(RKV_END),
        pl.BlockSpec((nb, C, 4 * LORA), lambda b, g, c: (b, c, 0)),
        vec(0), vec(RWKV_W), vec(2 * RWKV_W),
        vec(0), vec(0), vec(0), vec(0), vec(0), vec(0), vec(0),
        pl.BlockSpec((LORA, bw), lambda b, g, c: (0, g)),
        pl.BlockSpec((LORA, bw), lambda b, g, c: (0, g)),
        pl.BlockSpec((C, C), lambda b, g, c: (0, 0)),
        pl.BlockSpec((gw, gw), lambda b, g, c: (0, 0)),
    ]
    tri = jnp.asarray(np.tril(np.ones((C, C), np.float32)), BF16)
    seg = np.arange(gw) // RWKV_HEAD
    ones_bd = jnp.asarray((seg[:, None] == seg[None, :]).astype(np.float32), BF16)
    mu = prm["mu_rkv"].reshape(1, -1)
    args = [p0, p0, p0, p0, plora, mu, mu, mu, prm["w0"], prm["a0"], prm["k_k"], prm["k_a"], prm["r_k"],
            prm["gn_g"], prm["gn_b"], prm["w2"], prm["a2"], tri, ones_bd]
    state_spec = pl.BlockSpec((nb, gpb * RWKV_GROUP, RWKV_HEAD, RWKV_HEAD), lambda b, g, c: (b, g, 0, 0))
    has_state = s0 is not None
    if has_state:
        fcol = lambda off: pl.BlockSpec((nb, bw), lambda b, g, c: (b, off // bw + g))
        in_specs += [fcol(RKV_START), fcol(RKV_START + RWKV_W), fcol(RKV_START + 2 * RWKV_W),
                     pl.BlockSpec((nb, 4 * LORA), lambda b, g, c: (b, 0)), state_spec]
        args += [pfirst, pfirst, pfirst, pfirst_lora, s0]
    return pl.pallas_call(
        functools.partial(_rwkv_kernel, nb=nb, C=C, gpb=gpb, has_state=has_state),
        grid=(B // nb, ng, nc),
        in_specs=in_specs,
        out_specs=[pl.BlockSpec((nb, C, bw), lambda b, g, c: (b, c, g)), state_spec],
        out_shape=[jax.ShapeDtypeStruct((B, T, RWKV_W), BF16),
                   jax.ShapeDtypeStruct((B, RWKV_HEADS, RWKV_HEAD, RWKV_HEAD), F32)],
        scratch_shapes=[pltpu.VMEM((nb * gpb, gw, gw), F32), pltpu.VMEM((3, nb, 1, bw), F32),
                        pltpu.VMEM((nb, 1, 4 * LORA), F32)],
        compiler_params=_cparams(("parallel", "parallel", "arbitrary")),
        name="rwkv7",
    )(*args)


def _outproj0_kernel(x_ref, a1_ref, a2_ref, w1_ref, w2_ref, o_ref):
    acc = jnp.dot(a1_ref[...], w1_ref[...], preferred_element_type=F32)
    acc = acc + jnp.dot(a2_ref[...], w2_ref[...], preferred_element_type=F32)
    o_ref[...] = x_ref[...] + acc


def _outproj0(x, a1, a2, w, *, tm, tn):
    m, d = x.shape
    k1 = a1.shape[1]
    k2 = a2.shape[1]
    return pl.pallas_call(
        _outproj0_kernel,
        grid=(m // tm, d // tn),
        in_specs=[
            pl.BlockSpec((tm, tn), lambda i, j: (i, j)),
            pl.BlockSpec((tm, k1), lambda i, j: (i, 0)),
            pl.BlockSpec((tm, k2), lambda i, j: (i, 0)),
            pl.BlockSpec((k1, tn), lambda i, j: (0, j)),
            pl.BlockSpec((k2, tn), lambda i, j: (k1 // k2, j)),
        ],
        out_specs=pl.BlockSpec((tm, tn), lambda i, j: (i, j)),
        out_shape=jax.ShapeDtypeStruct((m, d), F32),
        compiler_params=_cparams(("parallel", "arbitrary")),
        name="outproj0",
    )(x, a1, a2, w, w)


def _outproj1_kernel(x_ref, y_ref, w_ref, g_ref, o_ref):
    out = x_ref[...] + jnp.dot(y_ref[...], w_ref[...], preferred_element_type=F32)
    o_ref[...] = out * lax.rsqrt(jnp.mean(out * out, axis=-1, keepdims=True) + NORM_EPS) * g_ref[...]


def _outproj1_norm(x, y, w, g, *, tm):
    m, d = x.shape
    k = y.shape[1]
    return pl.pallas_call(
        _outproj1_kernel,
        grid=(m // tm,),
        in_specs=[
            pl.BlockSpec((tm, d), lambda i: (i, 0)),
            pl.BlockSpec((tm, k), lambda i: (i, 0)),
            pl.BlockSpec((k, d), lambda i: (0, 0)),
            pl.BlockSpec((1, d), lambda i: (0, 0)),
        ],
        out_specs=pl.BlockSpec((tm, d), lambda i: (i, 0)),
        out_shape=jax.ShapeDtypeStruct((m, d), F32),
        compiler_params=_cparams(("parallel",)),
        name="outproj1_norm",
    )(x, y, w, g.reshape(1, d))


def _lru_gates(xc, wa_ref, wx_ref, ba, bx, spl):
    xcb = xc.astype(BF16)
    ra, rx = [], []
    for n in range(LRU_BLOCKS):
        blk = xcb[:, n * LRU_BW:(n + 1) * LRU_BW]
        ra.append(jnp.dot(blk, wa_ref[n], preferred_element_type=F32))
        rx.append(jnp.dot(blk, wx_ref[n], preferred_element_type=F32))
    gate_r = _sigmoid(jnp.concatenate(ra, axis=1) + ba)
    gate_i = _sigmoid(jnp.concatenate(rx, axis=1) + bx)
    log_a = -LRU_C * gate_r * spl
    a = jnp.exp(log_a)
    mult = jnp.sqrt(-jnp.tanh(log_a) * (a * a + 1.0))
    return a, mult, gate_i * xc


def _lru_prompt_kernel(xb_ref, gb_ref, cw_ref, cb_ref, wa_ref, wx_ref, ba_ref, bx_ref, lam_ref,
                       y_ref, hl_ref, xp_scr, a_scr, b_scr, h_scr, *, tt):
    i = pl.program_id(1)
    pad = 8

    @pl.when(i == 0)
    def _():
        xp_scr[0:pad, :] = jnp.zeros((pad, LRU_W), F32)
        h_scr[...] = jnp.zeros(h_scr.shape, F32)

    xp_scr[pad:pad + tt, :] = xb_ref[...]
    xc = cb_ref[...] + xp_scr[pad - 3:pad - 3 + tt, :] * cw_ref[0:1, :]
    for j in range(1, CONV_W):
        xc = xc + xp_scr[pad - 3 + j:pad - 3 + j + tt, :] * cw_ref[j:j + 1, :]
    xp_scr[0:pad, :] = xp_scr[tt:tt + pad, :]

    spl = _softplus(-lam_ref[...])
    a, mult, gx = _lru_gates(xc, wa_ref, wx_ref, ba_ref[...], bx_ref[...], spl)
    row = lax.broadcasted_iota(jnp.int32, (tt, LRU_W), 0)
    mult = jnp.where((row == 0) & (i == 0), 1.0, mult)
    a_scr[...] = a
    b_scr[...] = mult * gx

    def body(s, h):
        r0 = pl.multiple_of(s * 8, 8)
        a8 = a_scr[pl.ds(r0, 8), :]
        b8 = b_scr[pl.ds(r0, 8), :]
        out = []
        for j in range(8):
            h = a8[j:j + 1, :] * h + b8[j:j + 1, :]
            out.append(h)
        b_scr[pl.ds(r0, 8), :] = jnp.concatenate(out, axis=0)
        return h

    h = lax.fori_loop(0, tt // 8, body, h_scr[...])
    h_scr[...] = h
    gb = gb_ref[...]
    y_ref[...] = (b_scr[...] * (gb * _sigmoid(gb))).astype(BF16)

    @pl.when(i == pl.num_programs(1) - 1)
    def _():
        hl_ref[0] = h


def _lru_prompt(p1, B, T, prm, *, tt):
    nt = T // tt
    vec = pl.BlockSpec((1, LRU_W), lambda b, i: (0, 0))
    wspec = pl.BlockSpec((LRU_BLOCKS, LRU_BW, LRU_BW), lambda b, i: (0, 0, 0))
    return pl.pallas_call(
        functools.partial(_lru_prompt_kernel, tt=tt),
        grid=(B, nt),
        in_specs=[
            pl.BlockSpec((tt, LRU_W), lambda b, i: (b * nt + i, 0)),
            pl.BlockSpec((tt, LRU_W), lambda b, i: (b * nt + i, 1)),
            pl.BlockSpec((CONV_W, LRU_W), lambda b, i: (0, 0)),
            vec, wspec, wspec, vec, vec, vec,
        ],
        out_specs=[pl.BlockSpec((tt, LRU_W), lambda b, i: (b * nt + i, 0)),
                   pl.BlockSpec((1, 1, LRU_W), lambda b, i: (b, 0, 0))],
        out_shape=[jax.ShapeDtypeStruct((B * T, LRU_W), BF16), jax.ShapeDtypeStruct((B, 1, LRU_W), F32)],
        scratch_shapes=[pltpu.VMEM((tt + 8, LRU_W), F32), pltpu.VMEM((tt, LRU_W), F32),
                        pltpu.VMEM((tt, LRU_W), F32), pltpu.VMEM((1, LRU_W), F32)],
        compiler_params=_cparams(("parallel", "arbitrary")),
        name="lru_prompt",
    )(p1, p1, prm["conv_w"], prm["conv_b"], prm["wa"], prm["wx"], prm["ba"], prm["bx"], prm["lam"])


def _lru_sample_kernel(xb_ref, gb_ref, cbuf_ref, h0_ref, cw_ref, cb_ref, wa_ref, wx_ref, ba_ref, bx_ref, lam_ref,
                       y_ref, hl_ref, *, T):
    spl = _softplus(-lam_ref[...])
    h = h0_ref[...]
    for t in range(T):
        xc = cb_ref[...]
        for j in range(CONV_W):
            src = t + j - (CONV_W - 1)
            tap = xb_ref[src] if src >= 0 else cbuf_ref[src + CONV_W - 1]
            xc = xc + tap * cw_ref[j:j + 1, :]
        a, mult, gx = _lru_gates(xc, wa_ref, wx_ref, ba_ref[...], bx_ref[...], spl)
        h = a * h + mult * gx
        gb = gb_ref[t]
        y_ref[t] = (h * (gb * _sigmoid(gb))).astype(BF16)
    hl_ref[...] = h


def _lru_sample(p1_tm, cbuf_tm, h0, prm, *, bt):
    T, B, _ = p1_tm.shape
    vec = pl.BlockSpec((1, LRU_W), lambda i: (0, 0))
    wspec = pl.BlockSpec((LRU_BLOCKS, LRU_BW, LRU_BW), lambda i: (0, 0, 0))
    return pl.pallas_call(
        functools.partial(_lru_sample_kernel, T=T),
        grid=(B // bt,),
        in_specs=[
            pl.BlockSpec((T, bt, LRU_W), lambda i: (0, i, 0)),
            pl.BlockSpec((T, bt, LRU_W), lambda i: (0, i, 1)),
            pl.BlockSpec((CONV_W - 1, bt, LRU_W), lambda i: (0, i, 0)),
            pl.BlockSpec((bt, LRU_W), lambda i: (i, 0)),
            pl.BlockSpec((CONV_W, LRU_W), lambda i: (0, 0)),
            vec, wspec, wspec, vec, vec, vec,
        ],
        out_specs=[pl.BlockSpec((T, bt, LRU_W), lambda i: (0, i, 0)), pl.BlockSpec((bt, LRU_W), lambda i: (i, 0))],
        out_shape=[jax.ShapeDtypeStruct((T, B, LRU_W), BF16), jax.ShapeDtypeStruct((B, LRU_W), F32)],
        compiler_params=_cparams(("parallel",)),
        name="lru_sample",
    )(p1_tm, p1_tm, cbuf_tm, h0, prm["conv_w"], prm["conv_b"], prm["wa"], prm["wx"], prm["ba"], prm["bx"], prm["lam"])


def kernel(x_prompt, x_sample, state_ret, state_rwkv_wkv, state_rwkv_shift, state_lru_h, state_lru_conv,
           norm0_g, w_in0, w_out0, rwkv_mu_rkv, rwkv_mu_w, rwkv_mu_a, rwkv_w0, rwkv_w1, rwkv_w2,
           rwkv_a0, rwkv_a1, rwkv_a2, rwkv_k_k, rwkv_k_a, rwkv_r_k, rwkv_gn_g, rwkv_gn_b,
           norm1_g, w_in1, lru_conv_w, lru_conv_b, lru_wa, lru_ba, lru_wx, lru_bx, lru_lambda, w_out1,
           final_g):
    bp, tp, d = x_prompt.shape
    bs, ts, _ = x_sample.shape
    xp = x_prompt.reshape(bp * tp, d)
    xs = x_sample.reshape(bs * ts, d)

    mw = rwkv_mu_w[:, None]
    ma = rwkv_mu_a[:, None]
    w_lora = jnp.concatenate([(1.0 - mw) * rwkv_w1, mw * rwkv_w1, (1.0 - ma) * rwkv_a1, ma * rwkv_a1],
                             axis=1).astype(BF16)
    w_in0_b = w_in0.astype(BF16)
    w_out0_b = w_out0.astype(BF16)
    w_in1_b = w_in1.astype(BF16)
    w_out1_b = w_out1.astype(BF16)
    row = lambda v: v.reshape(1, -1)
    rwkv_prm = dict(mu_rkv=rwkv_mu_rkv, w0=row(rwkv_w0), a0=row(rwkv_a0), k_k=row(rwkv_k_k), k_a=row(rwkv_k_a),
                    r_k=row(rwkv_r_k), gn_g=row(rwkv_gn_g), gn_b=row(rwkv_gn_b),
                    w2=rwkv_w2.astype(BF16), a2=rwkv_a2.astype(BF16))
    lru_prm = dict(conv_w=lru_conv_w, conv_b=row(lru_conv_b), wa=lru_wa.astype(BF16), wx=lru_wx.astype(BF16),
                   ba=row(lru_ba), bx=row(lru_bx), lam=row(lru_lambda))

    p0p, plp = _norm_matmul(xp, norm0_g, w_in0_b, w_lora, apply_norm=True, tm=_tile(bp * tp, 1024), tn=512)
    p0s, pls = _norm_matmul(xs, norm0_g, w_in0_b, w_lora, apply_norm=True, tm=bs * ts, tn=512)
    pfirst, pfirst_lora = _norm_matmul(state_rwkv_shift, norm0_g, w_in0_b, w_lora, apply_norm=False, tm=bs, tn=512)
    p0p = p0p.reshape(bp, tp, IN0)
    p0s = p0s.reshape(bs, ts, IN0)

    cp = RET_CHUNK if tp % RET_CHUNK == 0 else tp
    cs = RET_CHUNK if ts % RET_CHUNK == 0 else ts
    ret_p, new_ret_p = _retention(p0p, bp, tp, cp, 1, 1, _ret_tables(np.arange(tp), cp), None)
    ret_s, new_ret_s = _retention(p0s, bs, ts, cs, 8, 2, _ret_tables(PAST_LEN + np.arange(ts), cs), state_ret)

    cwp = RWKV_CHUNK if tp % RWKV_CHUNK == 0 else tp
    rw_p, wkv_p = _rwkv(p0p, plp.reshape(bp, tp, 4 * LORA), bp, tp, cwp, bp, 2, None, None, None, rwkv_prm)
    rw_s, wkv_s = _rwkv(p0s, pls.reshape(bs, ts, 4 * LORA), bs, ts, ts, 8, 1, pfirst, pfirst_lora, state_rwkv_wkv,
                        rwkv_prm)
    ret_p = ret_p.reshape(bp * tp, RET_V)
    ret_s = ret_s.reshape(bs * ts, RET_V)
    rw_p = rw_p.reshape(bp * tp, RWKV_W)
    rw_s = rw_s.reshape(bs * ts, RWKV_W)

    x1p = _outproj0(xp, ret_p, rw_p, w_out0_b, tm=512, tn=1024)
    x1s = _outproj0(xs, ret_s, rw_s, w_out0_b, tm=512, tn=1024)

    last_rows = jnp.concatenate([x_prompt[:, -1, :], x_sample[:, -1, :]], axis=0)
    shift = _rmsnorm_rows(last_rows, norm0_g)

    p1p = _norm_matmul(x1p, norm1_g, w_in1_b, apply_norm=True, tm=_tile(bp * tp, 1024), tn=512)
    x1s_tm = x1s.reshape(bs, ts, d).transpose(1, 0, 2).reshape(ts * bs, d)
    p1s = _norm_matmul(x1s_tm, norm1_g, w_in1_b, apply_norm=True, tm=512, tn=512)

    yp, lh_p = _lru_prompt(p1p, bp, tp, lru_prm, tt=256)
    ys_tm, lh_s = _lru_sample(p1s.reshape(ts, bs, 2 * LRU_W), state_lru_conv.transpose(1, 0, 2), state_lru_h,
                              lru_prm, bt=64)

    out_p = _outproj1_norm(x1p, yp, w_out1_b, final_g, tm=256)
    out_s_tm = _outproj1_norm(x1s_tm, ys_tm.reshape(ts * bs, LRU_W), w_out1_b, final_g, tm=256)

    y_prompt = out_p.reshape(bp, tp, d)
    y_sample = out_s_tm.reshape(ts, bs, d).transpose(1, 0, 2)
    lc_p = p1p.reshape(bp, tp, 2 * LRU_W)[:, tp - (CONV_W - 1):, :LRU_W]
    lc_s = p1s.reshape(ts, bs, 2 * LRU_W)[ts - (CONV_W - 1):, :, :LRU_W].transpose(1, 0, 2)
    return (y_prompt, y_sample,
            new_ret_p, wkv_p, shift[:bp], lh_p.reshape(bp, LRU_W), lc_p,
            new_ret_s, wkv_s, shift[bp:], lh_s, lc_s)
```

```python
import functools

import numpy as np
import jax
import jax.numpy as jnp
from jax import lax
from jax.experimental import pallas as pl
from jax.experimental.pallas import tpu as pltpu

F32 = jnp.float32
BF16 = jnp.bfloat16

D_MODEL = 2048
PAST_LEN = 16384
RET_HEADS = 8
RET_DK = 64
RET_DV = 128
RET_QK = RET_HEADS * RET_DK
RET_V = RET_HEADS * RET_DV
RET_CHUNK = 128
ROPE_BASE = 10000.0
RWKV_HEAD = 64
RWKV_HEADS = 16
RWKV_W = RWKV_HEADS * RWKV_HEAD
LORA = 64
GN_EPS_RWKV = 64e-5
LRU_W = 2048
LRU_BLOCKS = 8
LRU_BW = LRU_W // LRU_BLOCKS
CONV_W = 4
LRU_C = 8.0
NORM_EPS = 1e-6
RKV_START = 2 * RET_QK + 2 * RET_V
RKV_END = RKV_START + 3 * RWKV_W
IN0 = RKV_END + RWKV_W

RWKV_GROUP = 4
GROUP_W = RWKV_GROUP * RWKV_HEAD
RWKV_CHUNK = 64

VMEM_LIMIT = 48 * 1024 * 1024


def _tile(m, pref):
    return pref if m % pref == 0 else m


def _round_robin(gens):
    gens = list(gens)
    while gens:
        alive = []
        for g in gens:
            try:
                next(g)
                alive.append(g)
            except StopIteration:
                pass
        gens = alive


def _cparams(sem):
    return pltpu.CompilerParams(dimension_semantics=sem, vmem_limit_bytes=VMEM_LIMIT)


def _split(x):
    hi = x.astype(BF16)
    lo = (x - hi.astype(F32)).astype(BF16)
    return hi, lo


_NN = (((1,), (0,)), ((), ()))
_NT = (((1,), (1,)), ((), ()))
_TN = (((0,), (0,)), ((), ()))


def _dot1(a, b, dims=_NN):
    return lax.dot_general(a.astype(BF16), b.astype(BF16), dims, preferred_element_type=F32)


def _dot_exact_rhs(a, b, dims=_NN):
    ah, al = _split(a)
    d = functools.partial(lax.dot_general, dimension_numbers=dims, preferred_element_type=F32)
    return d(ah, b) + d(al, b)


def _softplus(x):
    return jnp.maximum(x, 0.0) + jnp.log1p(jnp.exp(-jnp.abs(x)))


def _sigmoid(x):
    return 1.0 / (1.0 + jnp.exp(-x))


def _norm_matmul_kernel(x_ref, g_ref, w_ref, *rest, apply_norm, has_extra):
    if has_extra:
        we_ref, o_ref, oe_ref, h_ref = rest
    else:
        o_ref, h_ref = rest

    @pl.when(pl.program_id(1) == 0)
    def _():
        x = x_ref[...]
        if apply_norm:
            x = x * lax.rsqrt(jnp.mean(x * x, axis=-1, keepdims=True) + NORM_EPS) * g_ref[...]
        h_ref[...] = x.astype(BF16)
        if has_extra:
            oe_ref[...] = jnp.dot(h_ref[...], we_ref[...], preferred_element_type=F32)

    o_ref[...] = jnp.dot(h_ref[...], w_ref[...], preferred_element_type=F32)


def _norm_matmul(x, g, w, w_extra=None, *, apply_norm, tm, tn):
    m, d = x.shape
    n = w.shape[1]
    has_extra = w_extra is not None
    in_specs = [
        pl.BlockSpec((tm, d), lambda i, j: (i, 0)),
        pl.BlockSpec((1, d), lambda i, j: (0, 0)),
        pl.BlockSpec((d, tn), lambda i, j: (0, j)),
    ]
    out_specs = [pl.BlockSpec((tm, tn), lambda i, j: (i, j))]
    out_shape = [jax.ShapeDtypeStruct((m, n), F32)]
    args = [x, g.reshape(1, d), w]
    if has_extra:
        ne = w_extra.shape[1]
        in_specs.append(pl.BlockSpec((d, ne), lambda i, j: (0, 0)))
        out_specs.append(pl.BlockSpec((tm, ne), lambda i, j: (i, 0)))
        out_shape.append(jax.ShapeDtypeStruct((m, ne), F32))
        args.append(w_extra)
    out = pl.pallas_call(
        functools.partial(_norm_matmul_kernel, apply_norm=apply_norm, has_extra=has_extra),
        grid=(m // tm, n // tn),
        in_specs=in_specs,
        out_specs=out_specs,
        out_shape=out_shape,
        scratch_shapes=[pltpu.VMEM((tm, d), BF16)],
        compiler_params=_cparams(("parallel", "arbitrary")),
        name="norm_matmul",
    )(*args)
    return out if has_extra else out[0]


def _rmsnorm_kernel(x_ref, g_ref, o_ref):
    x = x_ref[...]
    o_ref[...] = x * lax.rsqrt(jnp.mean(x * x, axis=-1, keepdims=True) + NORM_EPS) * g_ref[...]


def _rmsnorm_rows(x, g):
    return pl.pallas_call(
        _rmsnorm_kernel,
        out_shape=jax.ShapeDtypeStruct(x.shape, F32),
        name="rmsnorm_rows",
    )(x, g.reshape(1, -1))


def _ret_kernel(q_ref, k_ref, v_ref, g_ref, cos_ref, sin_ref, dmask_ref, qdec_ref, kdec_ref, sdec_ref,
                *rest, nb, C, unroll, has_s0):
    if has_s0:
        s0_ref, o_ref, sout_ref, s_scr = rest
    else:
        o_ref, sout_ref, s_scr = rest
    c = pl.program_id(1)

    @pl.when(c == 0)
    def _():
        if has_s0:
            s_scr[...] = s0_ref[...]
        else:
            s_scr[...] = jnp.zeros(s_scr.shape, F32)

    pw = 2 * RET_DK
    lane = lax.broadcasted_iota(jnp.int32, (C, pw), 1)
    first_half = (lane & (RET_DK // 2)) == 0
    cos = cos_ref[...]
    sin = sin_ref[...]

    def rope(x):
        rot = jnp.where(first_half, pltpu.roll(x, pw - RET_DK // 2, 1), pltpu.roll(x, RET_DK // 2, 1))
        return x * cos + rot * sin

    def chain(i, hp):
        q = rope(q_ref[i, :, hp * pw:(hp + 1) * pw])
        k = rope(k_ref[i, :, hp * pw:(hp + 1) * pw]) * (RET_DK ** -0.5)
        kd = k * kdec_ref[hp]
        part = []
        for h2 in range(2):
            h = 2 * hp + h2
            qh = q[:, h2 * RET_DK:(h2 + 1) * RET_DK].astype(BF16)
            kh = k[:, h2 * RET_DK:(h2 + 1) * RET_DK].astype(BF16)
            kdh = kd[:, h2 * RET_DK:(h2 + 1) * RET_DK].astype(BF16)
            vh = v_ref[i, :, h * RET_DV:(h + 1) * RET_DV].astype(BF16)
            s_old = s_scr[i, h]
            sc = lax.dot_general(qh, kh, _NT, preferred_element_type=F32)
            cross = jnp.dot(qh, s_old.astype(BF16), preferred_element_type=F32)
            s_add = lax.dot_general(kdh, vh, _TN, preferred_element_type=F32)
            part.append((h, vh, s_old, sc, cross, s_add))
        yield
        outs = []
        for h, vh, s_old, sc, cross, s_add in part:
            s_scr[i, h] = s_old * sdec_ref[h] + s_add
            o = jnp.dot((sc * dmask_ref[h]).astype(BF16), vh, preferred_element_type=F32)
            outs.append((h, o, cross))
        yield
        for h, o, cross in outs:
            o = o + cross * qdec_ref[h]
            o = o * lax.rsqrt(jnp.mean(o * o, axis=-1, keepdims=True) + NORM_EPS)
            gate = g_ref[i, :, h * RET_DV:(h + 1) * RET_DV]
            o_ref[i, :, h * RET_DV:(h + 1) * RET_DV] = (o * (gate * _sigmoid(gate))).astype(BF16)

    if nb == unroll:
        _round_robin(chain(i, hp) for i in range(nb) for hp in range(RET_HEADS // 2))
    else:
        def body(j, carry):
            _round_robin(chain(j * unroll + u, hp) for u in range(unroll) for hp in range(RET_HEADS // 2))
            return carry
        lax.fori_loop(0, nb // unroll, body, 0)

    @pl.when(c == pl.num_programs(1) - 1)
    def _():
        sout_ref[...] = s_scr[...]


def _ret_tables(pos, C):
    inv = 1.0 / (ROPE_BASE ** (np.arange(0, RET_DK, 2, dtype=np.float64) / RET_DK))
    ang = pos.astype(np.float64)[:, None] * inv[None, :]
    cos = np.tile(np.cos(ang), (1, 4))
    sin = np.tile(np.concatenate([-np.sin(ang), np.sin(ang)], axis=1), (1, 2))
    log_gamma = np.log(1.0 - 2.0 ** (-5.0 - np.arange(RET_HEADS, dtype=np.float64)))
    n = np.arange(C, dtype=np.float64)
    diff = n[:, None] - n[None, :]
    dmask = np.where(diff >= 0, np.exp(log_gamma[:, None, None] * np.maximum(diff, 0.0)), 0.0)
    qdec = np.broadcast_to(np.exp(log_gamma[:, None] * (n + 1.0))[:, :, None], (RET_HEADS, C, RET_DV))
    kdec = np.exp(log_gamma[:, None] * (C - 1.0 - n))
    kdec = np.repeat(kdec.reshape(RET_HEADS // 2, 2, C).transpose(0, 2, 1), RET_DK, axis=2)
    sdec = np.broadcast_to(np.exp(log_gamma * C)[:, None, None], (RET_HEADS, 1, RET_DV))
    f = lambda a: jnp.asarray(np.ascontiguousarray(a), F32)
    return f(cos), f(sin), f(dmask), f(qdec), f(kdec), f(sdec)


def _retention(p0, B, T, C, nb, unroll, tables, s0):
    cos, sin, dmask, qdec, kdec, sdec = tables
    nc = T // C
    whole = lambda a: pl.BlockSpec(a.shape, lambda b, c: (0,) * a.ndim)
    in_specs = [
        pl.BlockSpec((nb, C, RET_QK), lambda b, c: (b, c, 0)),
        pl.BlockSpec((nb, C, RET_QK), lambda b, c: (b, c, 1)),
        pl.BlockSpec((nb, C, RET_V), lambda b, c: (b, c, 2 * RET_QK // RET_V)),
        pl.BlockSpec((nb, C, RET_V), lambda b, c: (b, c, 2 * RET_QK // RET_V + 1)),
        pl.BlockSpec((C, 2 * RET_DK), lambda b, c: (c, 0)),
        pl.BlockSpec((C, 2 * RET_DK), lambda b, c: (c, 0)),
        whole(dmask), whole(qdec), whole(kdec), whole(sdec),
    ]
    args = [p0, p0, p0, p0, cos, sin, dmask, qdec, kdec, sdec]
    state_spec = pl.BlockSpec((nb, RET_HEADS, RET_DK, RET_DV), lambda b, c: (b, 0, 0, 0))
    if s0 is not None:
        in_specs.append(state_spec)
        args.append(s0)
    return pl.pallas_call(
        functools.partial(_ret_kernel, nb=nb, C=C, unroll=unroll, has_s0=s0 is not None),
        grid=(B // nb, nc),
        in_specs=in_specs,
        out_specs=[pl.BlockSpec((nb, C, RET_V), lambda b, c: (b, c, 0)), state_spec],
        out_shape=[jax.ShapeDtypeStruct((B, T, RET_V), BF16),
                   jax.ShapeDtypeStruct((B, RET_HEADS, RET_DK, RET_DV), F32)],
        scratch_shapes=[pltpu.VMEM((nb, RET_HEADS, RET_DK, RET_DV), F32)],
        compiler_params=_cparams(("parallel", "arbitrary")),
        name="retention",
    )(*args)


def _expand_blocks(y, blk, groups):
    rows, w = y.shape
    if blk == RWKV_HEAD and w == 2 * 128 and groups == 4:
        lo = lax.broadcasted_iota(jnp.int32, (rows, 128), 1) < blk
        z = jnp.zeros((rows, 128), F32)
        y0, y1 = y[:, :128], y[:, 128:]
        return jnp.concatenate([
            jnp.concatenate([jnp.where(lo, y0, 0.0), z], axis=1),
            jnp.concatenate([jnp.where(lo, 0.0, y0), z], axis=1),
            jnp.concatenate([z, jnp.where(lo, y1, 0.0)], axis=1),
            jnp.concatenate([z, jnp.where(lo, 0.0, y1)], axis=1)], axis=0)
    shift_r = rows.bit_length() - 1
    shift_l = blk.bit_length() - 1
    mask = (lax.broadcasted_iota(jnp.int32, (groups * rows, w), 0) >> shift_r) == \
           (lax.broadcasted_iota(jnp.int32, (groups * rows, w), 1) >> shift_l)
    return jnp.where(mask, jnp.concatenate([y] * groups, axis=0), 0.0)


def _rwkv_kernel(pr_ref, pk_ref, pv_ref, pg_ref, pl_ref, mur_ref, muk_ref, muv_ref, w0_ref, a0_ref, kk_ref, ka_ref,
                 rk_ref, gng_ref, gnb_ref, w2_ref, a2_ref, tri_ref, ones_ref, *rest, nb, C, gpb, has_state):
    if has_state:
        fr_ref, fk_ref, fv_ref, fl_ref, s0_ref, o_ref, sout_ref, s_scr, prev_scr, prevl_scr = rest
    else:
        o_ref, sout_ref, s_scr, prev_scr, prevl_scr = rest
    c = pl.program_id(2)
    G = RWKV_GROUP
    GC = G * C
    log2c = C.bit_length() - 1
    pair_w = 2 * RWKV_HEAD
    bf = lambda x: x.astype(BF16)
    dot = functools.partial(lax.dot_general, preferred_element_type=F32)

    @pl.when(c == 0)
    def _():
        s_scr[...] = jnp.zeros(s_scr.shape, F32)
        prev_scr[...] = jnp.zeros(prev_scr.shape, F32)
        prevl_scr[...] = jnp.zeros(prevl_scr.shape, F32)
        if has_state:
            z = jnp.zeros((RWKV_HEAD, RWKV_HEAD), F32)
            for i in range(nb):
                for hg in range(gpb * G):
                    g = hg % G
                    head = s0_ref[i, hg]
                    blk = jnp.concatenate([head, z] if g % 2 == 0 else [z, head], axis=1)
                    s_scr[i * gpb + hg // G, g * RWKV_HEAD:(g + 1) * RWKV_HEAD,
                          (g // 2) * pair_w:(g // 2 + 1) * pair_w] = blk

    is_row0 = lax.broadcasted_iota(jnp.int32, (C, GROUP_W), 0) == 0
    t_idx = lax.broadcasted_iota(jnp.int32, (C, GC), 0)
    s_idx = lax.broadcasted_iota(jnp.int32, (C, GC), 1) & (C - 1)
    strict = s_idx < t_idx
    incl = s_idx <= t_idx
    eye_lb = jnp.where(s_idx == t_idx, 1.0, 0.0).astype(F32)
    ones_bd = ones_ref[...]

    def seg_sum(x):
        return _dot1(x, ones_bd)

    def lora_inputs(i):
        plo = pl_ref[i]
        prev = fl_ref[pl.ds(i, 1), :] if has_state else prevl_scr[i]
        plo_s = jnp.where(is_row0, prev, pltpu.roll(plo, 1, 0))
        if not has_state:
            prevl_scr[i] = plo[C - 1:C, :]
        pre_w = plo[:, 0:LORA] + plo_s[:, LORA:2 * LORA]
        pre_a = plo[:, 2 * LORA:3 * LORA] + plo_s[:, 3 * LORA:4 * LORA]
        return bf(jnp.tanh(pre_w)), bf(pre_a)

    lora = [lora_inputs(i) for i in range(nb)]

    def chain(i, gg):
        cols = slice(gg * GROUP_W, (gg + 1) * GROUP_W)

        def shifted(x, k, first_ref):
            prev = first_ref[pl.ds(i, 1), cols] if has_state else prev_scr[k, i, :, cols]
            return jnp.where(is_row0, prev, pltpu.roll(x, 1, 0))

        pr = pr_ref[i, :, cols]
        pk = pk_ref[i, :, cols]
        pv = pv_ref[i, :, cols]
        pr_s = shifted(pr, 0, fr_ref if has_state else None)
        pk_s = shifted(pk, 1, fk_ref if has_state else None)
        pv_s = shifted(pv, 2, fv_ref if has_state else None)
        if not has_state:
            prev_scr[0, i, :, cols] = pr[C - 1:C, :]
            prev_scr[1, i, :, cols] = pk[C - 1:C, :]
            prev_scr[2, i, :, cols] = pv[C - 1:C, :]

        r = pr + mur_ref[:, cols] * (pr_s - pr)
        kw = pk + muk_ref[:, cols] * (pk_s - pk)
        vw = pv + muv_ref[:, cols] * (pv_s - pv)
        wl = dot(lora[i][0], w2_ref[:, cols], _NN)
        al = dot(lora[i][1], a2_ref[:, cols], _NN)
        kk = kw * kk_ref[:, cols]
        kk_ss = seg_sum(kk * kk)
        yield
        w_log = -_softplus(-(w0_ref[:, cols] + wl)) - 0.5
        logw = -jnp.exp(w_log)
        cum = _dot_exact_rhs_lhs(tri_ref[...], logw)
        a = _sigmoid(a0_ref[:, cols] + al)
        kk = kk / jnp.maximum(jnp.sqrt(kk_ss), 1e-12)
        k = kw * (1.0 + (a - 1.0) * ka_ref[:, cols])
        bv = kk * a
        bonus = seg_sum(r * k * rk_ref[:, cols])
        v_bd = bf(_expand_blocks(vw, RWKV_HEAD, G))
        yield
        tot = cum[C - 1:C, :]
        e_neg = jnp.exp(-cum)
        e_rem = jnp.exp(tot - cum)
        a_t = -kk * jnp.exp(cum - logw)
        r_t = r * jnp.exp(cum)
        b_t = bv * e_neg
        k_t = k * e_neg
        b_h = bf(bv * e_rem)
        k_h = bf(k * e_rem)

        s_old = s_scr[i * gpb + gg]
        ar = bf(jnp.concatenate([a_t, r_t], axis=0))
        xb = dot(ar, bf(_expand_blocks(b_t, RWKV_HEAD, G)), _NT)
        xk = dot(ar, bf(_expand_blocks(k_t, RWKV_HEAD, G)), _NT)
        xs = dot(ar, bf(s_old), _NT)
        yield
        l_ab = jnp.where(strict, xb[:C], 0.0)
        l_ak = jnp.where(strict, xk[:C], 0.0)
        q = dot(bf(l_ab), bf(_expand_blocks(l_ab, C, G)), _NN)
        rhs = xs[:C] + dot(bf(l_ak), v_bd, _NN)
        p = eye_lb + l_ab
        m_rb = bf(jnp.where(incl, xb[C:], 0.0))
        m_rk = bf(jnp.where(incl, xk[C:], 0.0))
        yield
        for it in range(log2c - 1):
            q_bd = bf(_expand_blocks(q, C, G))
            if it < log2c - 2:
                pq = dot(bf(jnp.concatenate([p, q], axis=0)), q_bd, _NN)
                yield
                p = p + pq[:C]
                q = pq[C:]
            else:
                pq = dot(bf(p), q_bd, _NN)
                yield
                p = p + pq
        u = dot(bf(p), bf(_expand_blocks(rhs, RWKV_HEAD, G)), _NN)
        y_v = dot(m_rk, v_bd, _NN)
        s_v = dot(bf(vw), k_h, _TN)
        yield
        y_u = dot(m_rb, bf(_expand_blocks(u, RWKV_HEAD, G)), _NN)
        s_u = dot(bf(u), b_h, _TN)
        yield
        y = xs[C:] + y_u + y_v
        s_scr[i * gpb + gg] = s_old * jnp.exp(tot) + _expand_blocks_mask(s_u + s_v)
        inv_n = 1.0 / RWKV_HEAD
        mean = seg_sum(y) * inv_n
        yield
        d = y - mean
        var = seg_sum(d * d) * inv_n
        yield
        yn = d * lax.rsqrt(var + GN_EPS_RWKV) * gng_ref[:, cols] + gnb_ref[:, cols]
        yn = yn + bonus * vw
        gate = pg_ref[i, :, cols]
        o_ref[i, :, cols] = (yn * (gate * _sigmoid(gate))).astype(BF16)

    _round_robin(chain(i, gg) for i in range(nb) for gg in range(gpb))

    @pl.when(c == pl.num_programs(2) - 1)
    def _():
        for i in range(nb):
            for hg in range(gpb * G):
                lo = (hg % G) * RWKV_HEAD
                sout_ref[i, hg] = s_scr[i * gpb + hg // G, lo:lo + RWKV_HEAD, lo:lo + RWKV_HEAD]


def _expand_blocks_mask(s):
    lo = lax.broadcasted_iota(jnp.int32, (RWKV_HEAD, 128), 1) < RWKV_HEAD
    z = jnp.zeros((RWKV_HEAD, 128), F32)
    out = []
    for g in range(RWKV_GROUP):
        t = s[g * RWKV_HEAD:(g + 1) * RWKV_HEAD, (g // 2) * 128:(g // 2 + 1) * 128]
        t = jnp.where(lo, t, 0.0) if g % 2 == 0 else jnp.where(lo, 0.0, t)
        out.append(jnp.concatenate([t, z] if g < 2 else [z, t], axis=1))
    return jnp.concatenate(out, axis=0)


def _dot_exact_rhs_lhs(tri, x):
    xh, xl = _split(x)
    return jnp.dot(tri, xh, preferred_element_type=F32) + jnp.dot(tri, xl, preferred_element_type=F32)


def _rwkv(p0, plora, B, T, C, nb, gpb, pfirst, pfirst_lora, s0, prm):
    nc = T // C
    ng = RWKV_HEADS // (RWKV_GROUP * gpb)
    gw = GROUP_W
    bw = gpb * gw
    col = lambda off: pl.BlockSpec((nb, C, bw), lambda b, g, c: (b, c, off // bw + g))
    vec = lambda off: pl.BlockSpec((1, bw), lambda b, g, c: (0, off // bw + g))
    in_specs = [
        col(RKV_START), col(RKV_START + RWKV_W), col(RKV_START + 2 * RWKV_W), col(RKV_END),
        pl.BlockSpec((nb, C, 4 * LORA), lambda b, g, c: (b, c, 0)),
        vec(0), vec(RWKV_W), vec(2 * RWKV_W),
        vec(0), vec(0), vec(0), vec(0), vec(0), vec(0), vec(0),
        pl.BlockSpec((LORA, bw), lambda b, g, c: (0, g)),
        pl.BlockSpec((LORA, bw), lambda b, g, c: (0, g)),
        pl.BlockSpec((C, C), lambda b, g, c: (0, 0)),
        pl.BlockSpec((gw, gw), lambda b, g, c: (0, 0)),
    ]
    tri = jnp.asarray(np.tril(np.ones((C, C), np.float32)), BF16)
    seg = np.arange(gw) // RWKV_HEAD
    ones_bd = jnp.asarray((seg[:, None] == seg[None, :]).astype(np.float32), BF16)
    mu = prm["mu_rkv"].reshape(1, -1)
    args = [p0, p0, p0, p0, plora, mu, mu, mu, prm["w0"], prm["a0"], prm["k_k"], prm["k_a"], prm["r_k"],
            prm["gn_g"], prm["gn_b"], prm["w2"], prm["a2"], tri, ones_bd]
    state_spec = pl.BlockSpec((nb, gpb * RWKV_GROUP, RWKV_HEAD, RWKV_HEAD), lambda b, g, c: (b, g, 0, 0))
    has_state = s0 is not None
    if has_state:
        fcol = lambda off: pl.BlockSpec((nb, bw), lambda b, g, c: (b, off // bw + g))
        in_specs += [fcol(RKV_START), fcol(RKV_START + RWKV_W), fcol(RKV_START + 2 * RWKV_W),
                     pl.BlockSpec((nb, 4 * LORA), lambda b, g, c: (b, 0)), state_spec]
        args += [pfirst, pfirst, pfirst, pfirst_lora, s0]
    return pl.pallas_call(
        functools.partial(_rwkv_kernel, nb=nb, C=C, gpb=gpb, has_state=has_state),
        grid=(B // nb, ng, nc),
        in_specs=in_specs,
        out_specs=[pl.BlockSpec((nb, C, bw), lambda b, g, c: (b, c, g)), state_spec],
        out_shape=[jax.ShapeDtypeStruct((B, T, RWKV_W), BF16),
                   jax.ShapeDtypeStruct((B, RWKV_HEADS, RWKV_HEAD, RWKV_HEAD), F32)],
        scratch_shapes=[pltpu.VMEM((nb * gpb, gw, gw), F32), pltpu.VMEM((3, nb, 1, bw), F32),
                        pltpu.VMEM((nb, 1, 4 * LORA), F32)],
        compiler_params=_cparams(("parallel", "parallel", "arbitrary")),
        name="rwkv7",
    )(*args)


def _outproj0_kernel(x_ref, a1_ref, a2_ref, w1_ref, w2_ref, o_ref):
    acc = jnp.dot(a1_ref[...], w1_ref[...], preferred_element_type=F32)
    acc = acc + jnp.dot(a2_ref[...], w2_ref[...], preferred_element_type=F32)
    o_ref[...] = x_ref[...] + acc


def _outproj0(x, a1, a2, w, *, tm, tn):
    m, d = x.shape
    k1 = a1.shape[1]
    k2 = a2.shape[1]
    return pl.pallas_call(
        _outproj0_kernel,
        grid=(m // tm, d // tn),
        in_specs=[
            pl.BlockSpec((tm, tn), lambda i, j: (i, j)),
            pl.BlockSpec((tm, k1), lambda i, j: (i, 0)),
            pl.BlockSpec((tm, k2), lambda i, j: (i, 0)),
            pl.BlockSpec((k1, tn), lambda i, j: (0, j)),
            pl.BlockSpec((k2, tn), lambda i, j: (k1 // k2, j)),
        ],
        out_specs=pl.BlockSpec((tm, tn), lambda i, j: (i, j)),
        out_shape=jax.ShapeDtypeStruct((m, d), F32),
        compiler_params=_cparams(("parallel", "arbitrary")),
        name="outproj0",
    )(x, a1, a2, w, w)


def _outproj1_kernel(x_ref, y_ref, w_ref, g_ref, o_ref):
    out = x_ref[...] + jnp.dot(y_ref[...], w_ref[...], preferred_element_type=F32)
    o_ref[...] = out * lax.rsqrt(jnp.mean(out * out, axis=-1, keepdims=True) + NORM_EPS) * g_ref[...]


def _outproj1_norm(x, y, w, g, *, tm):
    m, d = x.shape
    k = y.shape[1]
    return pl.pallas_call(
        _outproj1_kernel,
        grid=(m // tm,),
        in_specs=[
            pl.BlockSpec((tm, d), lambda i: (i, 0)),
            pl.BlockSpec((tm, k), lambda i: (i, 0)),
            pl.BlockSpec((k, d), lambda i: (0, 0)),
            pl.BlockSpec((1, d), lambda i: (0, 0)),
        ],
        out_specs=pl.BlockSpec((tm, d), lambda i: (i, 0)),
        out_shape=jax.ShapeDtypeStruct((m, d), F32),
        compiler_params=_cparams(("parallel",)),
        name="outproj1_norm",
    )(x, y, w, g.reshape(1, d))


def _lru_gates(xc, wa_ref, wx_ref, ba, bx, spl):
    xcb = xc.astype(BF16)
    ra, rx = [], []
    for n in range(LRU_BLOCKS):
        blk = xcb[:, n * LRU_BW:(n + 1) * LRU_BW]
        ra.append(jnp.dot(blk, wa_ref[n], preferred_element_type=F32))
        rx.append(jnp.dot(blk, wx_ref[n], preferred_element_type=F32))
    gate_r = _sigmoid(jnp.concatenate(ra, axis=1) + ba)
    gate_i = _sigmoid(jnp.concatenate(rx, axis=1) + bx)
    log_a = -LRU_C * gate_r * spl
    a = jnp.exp(log_a)
    mult = jnp.sqrt(-jnp.tanh(log_a) * (a * a + 1.0))
    return a, mult, gate_i * xc


def _lru_prompt_kernel(xb_ref, gb_ref, cw_ref, cb_ref, wa_ref, wx_ref, ba_ref, bx_ref, lam_ref,
                       y_ref, hl_ref, xp_scr, a_scr, b_scr, h_scr, *, tt):
    i = pl.program_id(1)
    pad = 8

    @pl.when(i == 0)
    def _():
        xp_scr[0:pad, :] = jnp.zeros((pad, LRU_W), F32)
        h_scr[...] = jnp.zeros(h_scr.shape, F32)

    xp_scr[pad:pad + tt, :] = xb_ref[...]
    xc = cb_ref[...] + xp_scr[pad - 3:pad - 3 + tt, :] * cw_ref[0:1, :]
    for j in range(1, CONV_W):
        xc = xc + xp_scr[pad - 3 + j:pad - 3 + j + tt, :] * cw_ref[j:j + 1, :]
    xp_scr[0:pad, :] = xp_scr[tt:tt + pad, :]

    spl = _softplus(-lam_ref[...])
    a, mult, gx = _lru_gates(xc, wa_ref, wx_ref, ba_ref[...], bx_ref[...], spl)
    row = lax.broadcasted_iota(jnp.int32, (tt, LRU_W), 0)
    mult = jnp.where((row == 0) & (i == 0), 1.0, mult)
    a_scr[...] = a
    b_scr[...] = mult * gx

    def body(s, h):
        r0 = pl.multiple_of(s * 8, 8)
        a8 = a_scr[pl.ds(r0, 8), :]
        b8 = b_scr[pl.ds(r0, 8), :]
        out = []
        for j in range(8):
            h = a8[j:j + 1, :] * h + b8[j:j + 1, :]
            out.append(h)
        b_scr[pl.ds(r0, 8), :] = jnp.concatenate(out, axis=0)
        return h

    h = lax.fori_loop(0, tt // 8, body, h_scr[...], unroll=4)
    h_scr[...] = h
    gb = gb_ref[...]
    y_ref[...] = (b_scr[...] * (gb * _sigmoid(gb))).astype(BF16)

    @pl.when(i == pl.num_programs(1) - 1)
    def _():
        hl_ref[0] = h


def _lru_prompt(p1, B, T, prm, *, tt):
    nt = T // tt
    vec = pl.BlockSpec((1, LRU_W), lambda b, i: (0, 0))
    wspec = pl.BlockSpec((LRU_BLOCKS, LRU_BW, LRU_BW), lambda b, i: (0, 0, 0))
    return pl.pallas_call(
        functools.partial(_lru_prompt_kernel, tt=tt),
        grid=(B, nt),
        in_specs=[
            pl.BlockSpec((tt, LRU_W), lambda b, i: (b * nt + i, 0)),
            pl.BlockSpec((tt, LRU_W), lambda b, i: (b * nt + i, 1)),
            pl.BlockSpec((CONV_W, LRU_W), lambda b, i: (0, 0)),
            vec, wspec, wspec, vec, vec, vec,
        ],
        out_specs=[pl.BlockSpec((tt, LRU_W), lambda b, i: (b * nt + i, 0)),
                   pl.BlockSpec((1, 1, LRU_W), lambda b, i: (b, 0, 0))],
        out_shape=[jax.ShapeDtypeStruct((B * T, LRU_W), BF16), jax.ShapeDtypeStruct((B, 1, LRU_W), F32)],
        scratch_shapes=[pltpu.VMEM((tt + 8, LRU_W), F32), pltpu.VMEM((tt, LRU_W), F32),
                        pltpu.VMEM((tt, LRU_W), F32), pltpu.VMEM((1, LRU_W), F32)],
        compiler_params=_cparams(("parallel", "arbitrary")),
        name="lru_prompt",
    )(p1, p1, prm["conv_w"], prm["conv_b"], prm["wa"], prm["wx"], prm["ba"], prm["bx"], prm["lam"])


def _lru_sample_kernel(xb_ref, gb_ref, cbuf_ref, h0_ref, cw_ref, cb_ref, wa_ref, wx_ref, ba_ref, bx_ref, lam_ref,
                       y_ref, hl_ref, *, T):
    spl = _softplus(-lam_ref[...])
    h = h0_ref[...]
    for t in range(T):
        xc = cb_ref[...]
        for j in range(CONV_W):
            src = t + j - (CONV_W - 1)
            tap = xb_ref[src] if src >= 0 else cbuf_ref[src + CONV_W - 1]
            xc = xc + tap * cw_ref[j:j + 1, :]
        a, mult, gx = _lru_gates(xc, wa_ref, wx_ref, ba_ref[...], bx_ref[...], spl)
        h = a * h + mult * gx
        gb = gb_ref[t]
        y_ref[t] = (h * (gb * _sigmoid(gb))).astype(BF16)
    hl_ref[...] = h


def _lru_sample(p1_tm, cbuf_tm, h0, prm, *, bt):
    T, B, _ = p1_tm.shape
    vec = pl.BlockSpec((1, LRU_W), lambda i: (0, 0))
    wspec = pl.BlockSpec((LRU_BLOCKS, LRU_BW, LRU_BW), lambda i: (0, 0, 0))
    return pl.pallas_call(
        functools.partial(_lru_sample_kernel, T=T),
        grid=(B // bt,),
        in_specs=[
            pl.BlockSpec((T, bt, LRU_W), lambda i: (0, i, 0)),
            pl.BlockSpec((T, bt, LRU_W), lambda i: (0, i, 1)),
            pl.BlockSpec((CONV_W - 1, bt, LRU_W), lambda i: (0, i, 0)),
            pl.BlockSpec((bt, LRU_W), lambda i: (i, 0)),
            pl.BlockSpec((CONV_W, LRU_W), lambda i: (0, 0)),
            vec, wspec, wspec, vec, vec, vec,
        ],
        out_specs=[pl.BlockSpec((T, bt, LRU_W), lambda i: (0, i, 0)), pl.BlockSpec((bt, LRU_W), lambda i: (i, 0))],
        out_shape=[jax.ShapeDtypeStruct((T, B, LRU_W), BF16), jax.ShapeDtypeStruct((B, LRU_W), F32)],
        compiler_params=_cparams(("parallel",)),
        name="lru_sample",
    )(p1_tm, p1_tm, cbuf_tm, h0, prm["conv_w"], prm["conv_b"], prm["wa"], prm["wx"], prm["ba"], prm["bx"], prm["lam"])


def kernel(x_prompt, x_sample, state_ret, state_rwkv_wkv, state_rwkv_shift, state_lru_h, state_lru_conv,
           norm0_g, w_in0, w_out0, rwkv_mu_rkv, rwkv_mu_w, rwkv_mu_a, rwkv_w0, rwkv_w1, rwkv_w2,
           rwkv_a0, rwkv_a1, rwkv_a2, rwkv_k_k, rwkv_k_a, rwkv_r_k, rwkv_gn_g, rwkv_gn_b,
           norm1_g, w_in1, lru_conv_w, lru_conv_b, lru_wa, lru_ba, lru_wx, lru_bx, lru_lambda, w_out1,
           final_g):
    bp, tp, d = x_prompt.shape
    bs, ts, _ = x_sample.shape
    xp = x_prompt.reshape(bp * tp, d)
    xs = x_sample.reshape(bs * ts, d)

    mw = rwkv_mu_w[:, None]
    ma = rwkv_mu_a[:, None]
    w_lora = jnp.concatenate([(1.0 - mw) * rwkv_w1, mw * rwkv_w1, (1.0 - ma) * rwkv_a1, ma * rwkv_a1],
                             axis=1).astype(BF16)
    w_in0_b = w_in0.astype(BF16)
    w_out0_b = w_out0.astype(BF16)
    w_in1_b = w_in1.astype(BF16)
    w_out1_b = w_out1.astype(BF16)
    row = lambda v: v.reshape(1, -1)
    rwkv_prm = dict(mu_rkv=rwkv_mu_rkv, w0=row(rwkv_w0), a0=row(rwkv_a0), k_k=row(rwkv_k_k), k_a=row(rwkv_k_a),
                    r_k=row(rwkv_r_k), gn_g=row(rwkv_gn_g), gn_b=row(rwkv_gn_b),
                    w2=rwkv_w2.astype(BF16), a2=rwkv_a2.astype(BF16))
    lru_prm = dict(conv_w=lru_conv_w, conv_b=row(lru_conv_b), wa=lru_wa.astype(BF16), wx=lru_wx.astype(BF16),
                   ba=row(lru_ba), bx=row(lru_bx), lam=row(lru_lambda))

    p0p, plp = _norm_matmul(xp, norm0_g, w_in0_b, w_lora, apply_norm=True, tm=_tile(bp * tp, 1024), tn=512)
    p0s, pls = _norm_matmul(xs, norm0_g, w_in0_b, w_lora, apply_norm=True, tm=bs * ts, tn=512)
    pfirst, pfirst_lora = _norm_matmul(state_rwkv_shift, norm0_g, w_in0_b, w_lora, apply_norm=False, tm=bs, tn=512)
    p0p = p0p.reshape(bp, tp, IN0)
    p0s = p0s.reshape(bs, ts, IN0)

    cp = RET_CHUNK if tp % RET_CHUNK == 0 else tp
    cs = RET_CHUNK if ts % RET_CHUNK == 0 else ts
    ret_p, new_ret_p = _retention(p0p, bp, tp, cp, 1, 1, _ret_tables(np.arange(tp), cp), None)
    ret_s, new_ret_s = _retention(p0s, bs, ts, cs, 8, 2, _ret_tables(PAST_LEN + np.arange(ts), cs), state_ret)

    cwp = RWKV_CHUNK if tp % RWKV_CHUNK == 0 else tp
    rw_p, wkv_p = _rwkv(p0p, plp.reshape(bp, tp, 4 * LORA), bp, tp, cwp, bp, 4, None, None, None, rwkv_prm)
    rw_s, wkv_s = _rwkv(p0s, pls.reshape(bs, ts, 4 * LORA), bs, ts, ts, 8, 2, pfirst, pfirst_lora, state_rwkv_wkv,
                        rwkv_prm)
    ret_p = ret_p.reshape(bp * tp, RET_V)
    ret_s = ret_s.reshape(bs * ts, RET_V)
    rw_p = rw_p.reshape(bp * tp, RWKV_W)
    rw_s = rw_s.reshape(bs * ts, RWKV_W)

    x1p = _outproj0(xp, ret_p, rw_p, w_out0_b, tm=_tile(bp * tp, 1024), tn=1024)
    x1s = _outproj0(xs, ret_s, rw_s, w_out0_b, tm=512, tn=1024)

    last_rows = jnp.concatenate([x_prompt[:, -1, :], x_sample[:, -1, :]], axis=0)
    shift = _rmsnorm_rows(last_rows, norm0_g)

    p1p = _norm_matmul(x1p, norm1_g, w_in1_b, apply_norm=True, tm=_tile(bp * tp, 1024), tn=512)
    x1s_tm = x1s.reshape(bs, ts, d).transpose(1, 0, 2).reshape(ts * bs, d)
    p1s = _norm_matmul(x1s_tm, norm1_g, w_in1_b, apply_norm=True, tm=512, tn=512)

    yp, lh_p = _lru_prompt(p1p, bp, tp, lru_prm, tt=256)
    ys_tm, lh_s = _lru_sample(p1s.reshape(ts, bs, 2 * LRU_W), state_lru_conv.transpose(1, 0, 2), state_lru_h,
                              lru_prm, bt=64)

    out_p = _outproj1_norm(x1p, yp, w_out1_b, final_g, tm=512)
    out_s_tm = _outproj1_norm(x1s_tm, ys_tm.reshape(ts * bs, LRU_W), w_out1_b, final_g, tm=256)

    y_prompt = out_p.reshape(bp, tp, d)
    y_sample = out_s_tm.reshape(ts, bs, d).transpose(1, 0, 2)
    lc_p = p1p.reshape(bp, tp, 2 * LRU_W)[:, tp - (CONV_W - 1):, :LRU_W]
    lc_s = p1s.reshape(ts, bs, 2 * LRU_W)[ts - (CONV_W - 1):, :, :LRU_W].transpose(1, 0, 2)
    return (y_prompt, y_sample,
            new_ret_p, wkv_p, shift[:bp], lh_p.reshape(bp, LRU_W), lc_p,
            new_ret_s, wkv_s, shift[bp:], lh_s, lc_s)
```

```python
import functools

import numpy as np
import jax
import jax.numpy as jnp
from jax import lax
from jax.experimental import pallas as pl
from jax.experimental.pallas import tpu as pltpu

F32 = jnp.float32
BF16 = jnp.bfloat16

D_MODEL = 2048
PAST_LEN = 16384
RET_HEADS = 8
RET_DK = 64
RET_DV = 128
RET_QK = RET_HEADS * RET_DK
RET_V = RET_HEADS * RET_DV
RET_CHUNK = 128
ROPE_BASE = 10000.0
RWKV_HEAD = 64
RWKV_HEADS = 16
RWKV_W = RWKV_HEADS * RWKV_HEAD
LORA = 64
GN_EPS_RWKV = 64e-5
LRU_W = 2048
LRU_BLOCKS = 8
LRU_BW = LRU_W // LRU_BLOCKS
CONV_W = 4
LRU_C = 8.0
NORM_EPS = 1e-6
RKV_START = 2 * RET_QK + 2 * RET_V
RKV_END = RKV_START + 3 * RWKV_W
IN0 = RKV_END + RWKV_W

RWKV_GROUP = 4
GROUP_W = RWKV_GROUP * RWKV_HEAD
RWKV_CHUNK = 64

VMEM_LIMIT = 48 * 1024 * 1024


def _tile(m, pref):
    return pref if m % pref == 0 else m


def _round_robin(gens):
    gens = list(gens)
    while gens:
        alive = []
        for g in gens:
            try:
                next(g)
                alive.append(g)
            except StopIteration:
                pass
        gens = alive


def _cparams(sem):
    return pltpu.CompilerParams(dimension_semantics=sem, vmem_limit_bytes=VMEM_LIMIT)


def _split(x):
    hi = x.astype(BF16)
    lo = (x - hi.astype(F32)).astype(BF16)
    return hi, lo


_NN = (((1,), (0,)), ((), ()))
_NT = (((1,), (1,)), ((), ()))
_TN = (((0,), (0,)), ((), ()))


def _dot1(a, b, dims=_NN):
    return lax.dot_general(a.astype(BF16), b.astype(BF16), dims, preferred_element_type=F32)


def _dot_exact_rhs(a, b, dims=_NN):
    ah, al = _split(a)
    d = functools.partial(lax.dot_general, dimension_numbers=dims, preferred_element_type=F32)
    return d(ah, b) + d(al, b)


def _softplus(x):
    return jnp.maximum(x, 0.0) + jnp.log1p(jnp.exp(-jnp.abs(x)))


def _sigmoid(x):
    return 1.0 / (1.0 + jnp.exp(-x))


def _norm_matmul_kernel(x_ref, g_ref, w_ref, *rest, apply_norm, has_extra):
    if has_extra:
        we_ref, o_ref, oe_ref, h_ref = rest
    else:
        o_ref, h_ref = rest

    @pl.when(pl.program_id(1) == 0)
    def _():
        x = x_ref[...]
        if apply_norm:
            x = x * lax.rsqrt(jnp.mean(x * x, axis=-1, keepdims=True) + NORM_EPS) * g_ref[...]
        h_ref[...] = x.astype(BF16)
        if has_extra:
            oe_ref[...] = jnp.dot(h_ref[...], we_ref[...], preferred_element_type=F32)

    o_ref[...] = jnp.dot(h_ref[...], w_ref[...], preferred_element_type=F32)


def _norm_matmul(x, g, w, w_extra=None, *, apply_norm, tm, tn):
    m, d = x.shape
    n = w.shape[1]
    has_extra = w_extra is not None
    in_specs = [
        pl.BlockSpec((tm, d), lambda i, j: (i, 0)),
        pl.BlockSpec((1, d), lambda i, j: (0, 0)),
        pl.BlockSpec((d, tn), lambda i, j: (0, j)),
    ]
    out_specs = [pl.BlockSpec((tm, tn), lambda i, j: (i, j))]
    out_shape = [jax.ShapeDtypeStruct((m, n), F32)]
    args = [x, g.reshape(1, d), w]
    if has_extra:
        ne = w_extra.shape[1]
        in_specs.append(pl.BlockSpec((d, ne), lambda i, j: (0, 0)))
        out_specs.append(pl.BlockSpec((tm, ne), lambda i, j: (i, 0)))
        out_shape.append(jax.ShapeDtypeStruct((m, ne), F32))
        args.append(w_extra)
    out = pl.pallas_call(
        functools.partial(_norm_matmul_kernel, apply_norm=apply_norm, has_extra=has_extra),
        grid=(m // tm, n // tn),
        in_specs=in_specs,
        out_specs=out_specs,
        out_shape=out_shape,
        scratch_shapes=[pltpu.VMEM((tm, d), BF16)],
        compiler_params=_cparams(("parallel", "arbitrary")),
        name="norm_matmul",
    )(*args)
    return out if has_extra else out[0]


def _rmsnorm_kernel(x_ref, g_ref, o_ref):
    x = x_ref[...]
    o_ref[...] = x * lax.rsqrt(jnp.mean(x * x, axis=-1, keepdims=True) + NORM_EPS) * g_ref[...]


def _rmsnorm_rows(x, g):
    return pl.pallas_call(
        _rmsnorm_kernel,
        out_shape=jax.ShapeDtypeStruct(x.shape, F32),
        name="rmsnorm_rows",
    )(x, g.reshape(1, -1))


def _ret_kernel(q_ref, k_ref, v_ref, g_ref, cos_ref, sin_ref, dmask_ref, qdec_ref, kdec_ref, sdec_ref,
                *rest, nb, C, unroll, has_s0):
    if has_s0:
        s0_ref, o_ref, sout_ref, s_scr = rest
    else:
        o_ref, sout_ref, s_scr = rest
    c = pl.program_id(1)

    @pl.when(c == 0)
    def _():
        if has_s0:
            s_scr[...] = s0_ref[...]
        else:
            s_scr[...] = jnp.zeros(s_scr.shape, F32)

    pw = 2 * RET_DK
    lane = lax.broadcasted_iota(jnp.int32, (C, pw), 1)
    first_half = (lane & (RET_DK // 2)) == 0
    cos = cos_ref[...]
    sin = sin_ref[...]

    def rope(x):
        rot = jnp.where(first_half, pltpu.roll(x, pw - RET_DK // 2, 1), pltpu.roll(x, RET_DK // 2, 1))
        return x * cos + rot * sin

    def chain(i, hp):
        q = rope(q_ref[i, :, hp * pw:(hp + 1) * pw])
        k = rope(k_ref[i, :, hp * pw:(hp + 1) * pw]) * (RET_DK ** -0.5)
        kd = k * kdec_ref[hp]
        part = []
        for h2 in range(2):
            h = 2 * hp + h2
            qh = q[:, h2 * RET_DK:(h2 + 1) * RET_DK].astype(BF16)
            kh = k[:, h2 * RET_DK:(h2 + 1) * RET_DK].astype(BF16)
            kdh = kd[:, h2 * RET_DK:(h2 + 1) * RET_DK].astype(BF16)
            vh = v_ref[i, :, h * RET_DV:(h + 1) * RET_DV].astype(BF16)
            s_old = s_scr[i, h]
            sc = lax.dot_general(qh, kh, _NT, preferred_element_type=F32)
            cross = jnp.dot(qh, s_old.astype(BF16), preferred_element_type=F32)
            s_add = lax.dot_general(kdh, vh, _TN, preferred_element_type=F32)
            part.append((h, vh, s_old, sc, cross, s_add))
        yield
        outs = []
        for h, vh, s_old, sc, cross, s_add in part:
            s_scr[i, h] = s_old * sdec_ref[h] + s_add
            o = jnp.dot((sc * dmask_ref[h]).astype(BF16), vh, preferred_element_type=F32)
            outs.append((h, o, cross))
        yield
        for h, o, cross in outs:
            o = o + cross * qdec_ref[h]
            o = o * lax.rsqrt(jnp.mean(o * o, axis=-1, keepdims=True) + NORM_EPS)
            gate = g_ref[i, :, h * RET_DV:(h + 1) * RET_DV]
            o_ref[i, :, h * RET_DV:(h + 1) * RET_DV] = (o * (gate * _sigmoid(gate))).astype(BF16)

    if nb == unroll:
        _round_robin(chain(i, hp) for i in range(nb) for hp in range(RET_HEADS // 2))
    else:
        def body(j, carry):
            _round_robin(chain(j * unroll + u, hp) for u in range(unroll) for hp in range(RET_HEADS // 2))
            return carry
        lax.fori_loop(0, nb // unroll, body, 0)

    @pl.when(c == pl.num_programs(1) - 1)
    def _():
        sout_ref[...] = s_scr[...]


def _ret_tables(pos, C):
    inv = 1.0 / (ROPE_BASE ** (np.arange(0, RET_DK, 2, dtype=np.float64) / RET_DK))
    ang = pos.astype(np.float64)[:, None] * inv[None, :]
    cos = np.tile(np.cos(ang), (1, 4))
    sin = np.tile(np.concatenate([-np.sin(ang), np.sin(ang)], axis=1), (1, 2))
    log_gamma = np.log(1.0 - 2.0 ** (-5.0 - np.arange(RET_HEADS, dtype=np.float64)))
    n = np.arange(C, dtype=np.float64)
    diff = n[:, None] - n[None, :]
    dmask = np.where(diff >= 0, np.exp(log_gamma[:, None, None] * np.maximum(diff, 0.0)), 0.0)
    qdec = np.broadcast_to(np.exp(log_gamma[:, None] * (n + 1.0))[:, :, None], (RET_HEADS, C, RET_DV))
    kdec = np.exp(log_gamma[:, None] * (C - 1.0 - n))
    kdec = np.repeat(kdec.reshape(RET_HEADS // 2, 2, C).transpose(0, 2, 1), RET_DK, axis=2)
    sdec = np.broadcast_to(np.exp(log_gamma * C)[:, None, None], (RET_HEADS, 1, RET_DV))
    f = lambda a: jnp.asarray(np.ascontiguousarray(a), F32)
    return f(cos), f(sin), f(dmask), f(qdec), f(kdec), f(sdec)


def _retention(p0, B, T, C, nb, unroll, tables, s0):
    cos, sin, dmask, qdec, kdec, sdec = tables
    nc = T // C
    whole = lambda a: pl.BlockSpec(a.shape, lambda b, c: (0,) * a.ndim)
    in_specs = [
        pl.BlockSpec((nb, C, RET_QK), lambda b, c: (b, c, 0)),
        pl.BlockSpec((nb, C, RET_QK), lambda b, c: (b, c, 1)),
        pl.BlockSpec((nb, C, RET_V), lambda b, c: (b, c, 2 * RET_QK // RET_V)),
        pl.BlockSpec((nb, C, RET_V), lambda b, c: (b, c, 2 * RET_QK // RET_V + 1)),
        pl.BlockSpec((C, 2 * RET_DK), lambda b, c: (c, 0)),
        pl.BlockSpec((C, 2 * RET_DK), lambda b, c: (c, 0)),
        whole(dmask), whole(qdec), whole(kdec), whole(sdec),
    ]
    args = [p0, p0, p0, p0, cos, sin, dmask, qdec, kdec, sdec]
    state_spec = pl.BlockSpec((nb, RET_HEADS, RET_DK, RET_DV), lambda b, c: (b, 0, 0, 0))
    if s0 is not None:
        in_specs.append(state_spec)
        args.append(s0)
    return pl.pallas_call(
        functools.partial(_ret_kernel, nb=nb, C=C, unroll=unroll, has_s0=s0 is not None),
        grid=(B // nb, nc),
        in_specs=in_specs,
        out_specs=[pl.BlockSpec((nb, C, RET_V), lambda b, c: (b, c, 0)), state_spec],
        out_shape=[jax.ShapeDtypeStruct((B, T, RET_V), BF16),
                   jax.ShapeDtypeStruct((B, RET_HEADS, RET_DK, RET_DV), F32)],
        scratch_shapes=[pltpu.VMEM((nb, RET_HEADS, RET_DK, RET_DV), F32)],
        compiler_params=_cparams(("parallel", "arbitrary")),
        name="retention",
    )(*args)


def _expand_blocks(y, blk, groups):
    rows, w = y.shape
    if blk == RWKV_HEAD and w == 2 * 128 and groups == 4:
        lo = lax.broadcasted_iota(jnp.int32, (rows, 128), 1) < blk
        z = jnp.zeros((rows, 128), F32)
        y0, y1 = y[:, :128], y[:, 128:]
        return jnp.concatenate([
            jnp.concatenate([jnp.where(lo, y0, 0.0), z], axis=1),
            jnp.concatenate([jnp.where(lo, 0.0, y0), z], axis=1),
            jnp.concatenate([z, jnp.where(lo, y1, 0.0)], axis=1),
            jnp.concatenate([z, jnp.where(lo, 0.0, y1)], axis=1)], axis=0)
    shift_r = rows.bit_length() - 1
    shift_l = blk.bit_length() - 1
    mask = (lax.broadcasted_iota(jnp.int32, (groups * rows, w), 0) >> shift_r) == \
           (lax.broadcasted_iota(jnp.int32, (groups * rows, w), 1) >> shift_l)
    return jnp.where(mask, jnp.concatenate([y] * groups, axis=0), 0.0)


def _rwkv_kernel(pr_ref, pk_ref, pv_ref, pg_ref, pl_ref, mur_ref, muk_ref, muv_ref, w0_ref, a0_ref, kk_ref, ka_ref,
                 rk_ref, gng_ref, gnb_ref, w2_ref, a2_ref, tri_ref, ones_ref, *rest, nb, C, gpb, has_state):
    if has_state:
        fr_ref, fk_ref, fv_ref, fl_ref, s0_ref, o_ref, sout_ref, s_scr, prev_scr, prevl_scr = rest
    else:
        o_ref, sout_ref, s_scr, prev_scr, prevl_scr = rest
    c = pl.program_id(2)
    G = RWKV_GROUP
    GC = G * C
    log2c = C.bit_length() - 1
    pair_w = 2 * RWKV_HEAD
    bf = lambda x: x.astype(BF16)
    dot = functools.partial(lax.dot_general, preferred_element_type=F32)

    @pl.when(c == 0)
    def _():
        s_scr[...] = jnp.zeros(s_scr.shape, F32)
        prev_scr[...] = jnp.zeros(prev_scr.shape, F32)
        prevl_scr[...] = jnp.zeros(prevl_scr.shape, F32)
        if has_state:
            z = jnp.zeros((RWKV_HEAD, RWKV_HEAD), F32)
            for i in range(nb):
                for hg in range(gpb * G):
                    g = hg % G
                    head = s0_ref[i, hg]
                    blk = jnp.concatenate([head, z] if g % 2 == 0 else [z, head], axis=1)
                    s_scr[i * gpb + hg // G, g * RWKV_HEAD:(g + 1) * RWKV_HEAD,
                          (g // 2) * pair_w:(g // 2 + 1) * pair_w] = blk

    is_row0 = lax.broadcasted_iota(jnp.int32, (C, GROUP_W), 0) == 0
    t_idx = lax.broadcasted_iota(jnp.int32, (C, GC), 0)
    s_idx = lax.broadcasted_iota(jnp.int32, (C, GC), 1) & (C - 1)
    strict = s_idx < t_idx
    incl = s_idx <= t_idx
    eye_lb = jnp.where(s_idx == t_idx, 1.0, 0.0).astype(F32)
    ones_bd = ones_ref[...]

    def seg_sum(x):
        return _dot1(x, ones_bd)

    def lora_inputs(i):
        plo = pl_ref[i]
        prev = fl_ref[pl.ds(i, 1), :] if has_state else prevl_scr[i]
        plo_s = jnp.where(is_row0, prev, pltpu.roll(plo, 1, 0))
        if not has_state:
            prevl_scr[i] = plo[C - 1:C, :]
        pre_w = plo[:, 0:LORA] + plo_s[:, LORA:2 * LORA]
        pre_a = plo[:, 2 * LORA:3 * LORA] + plo_s[:, 3 * LORA:4 * LORA]
        return bf(jnp.tanh(pre_w)), bf(pre_a)

    lora = [lora_inputs(i) for i in range(nb)]

    def chain(i, gg):
        cols = slice(gg * GROUP_W, (gg + 1) * GROUP_W)

        def shifted(x, k, first_ref):
            prev = first_ref[pl.ds(i, 1), cols] if has_state else prev_scr[k, i, :, cols]
            return jnp.where(is_row0, prev, pltpu.roll(x, 1, 0))

        pr = pr_ref[i, :, cols]
        pk = pk_ref[i, :, cols]
        pv = pv_ref[i, :, cols]
        pr_s = shifted(pr, 0, fr_ref if has_state else None)
        pk_s = shifted(pk, 1, fk_ref if has_state else None)
        pv_s = shifted(pv, 2, fv_ref if has_state else None)
        if not has_state:
            prev_scr[0, i, :, cols] = pr[C - 1:C, :]
            prev_scr[1, i, :, cols] = pk[C - 1:C, :]
            prev_scr[2, i, :, cols] = pv[C - 1:C, :]

        r = pr + mur_ref[:, cols] * (pr_s - pr)
        kw = pk + muk_ref[:, cols] * (pk_s - pk)
        vw = pv + muv_ref[:, cols] * (pv_s - pv)
        wl = dot(lora[i][0], w2_ref[:, cols], _NN)
        al = dot(lora[i][1], a2_ref[:, cols], _NN)
        kk = kw * kk_ref[:, cols]
        kk_ss = seg_sum(kk * kk)
        yield
        w_log = -_softplus(-(w0_ref[:, cols] + wl)) - 0.5
        logw = -jnp.exp(w_log)
        cum = _dot_exact_rhs_lhs(tri_ref[...], logw)
        a = _sigmoid(a0_ref[:, cols] + al)
        kk = kk / jnp.maximum(jnp.sqrt(kk_ss), 1e-12)
        k = kw * (1.0 + (a - 1.0) * ka_ref[:, cols])
        bv = kk * a
        bonus = seg_sum(r * k * rk_ref[:, cols])
        v_bd = bf(_expand_blocks(vw, RWKV_HEAD, G))
        yield
        tot = cum[C - 1:C, :]
        e_neg = jnp.exp(-cum)
        e_rem = jnp.exp(tot - cum)
        a_t = -kk * jnp.exp(cum - logw)
        r_t = r * jnp.exp(cum)
        b_t = bv * e_neg
        k_t = k * e_neg
        b_h = bf(bv * e_rem)
        k_h = bf(k * e_rem)

        s_old = s_scr[i * gpb + gg]
        ar = bf(jnp.concatenate([a_t, r_t], axis=0))
        xb = dot(ar, bf(_expand_blocks(b_t, RWKV_HEAD, G)), _NT)
        xk = dot(ar, bf(_expand_blocks(k_t, RWKV_HEAD, G)), _NT)
        xs = dot(ar, bf(s_old), _NT)
        yield
        l_ab = jnp.where(strict, xb[:C], 0.0)
        l_ak = jnp.where(strict, xk[:C], 0.0)
        q = dot(bf(l_ab), bf(_expand_blocks(l_ab, C, G)), _NN)
        rhs = xs[:C] + dot(bf(l_ak), v_bd, _NN)
        p = eye_lb + l_ab
        m_rb = bf(jnp.where(incl, xb[C:], 0.0))
        m_rk = bf(jnp.where(incl, xk[C:], 0.0))
        yield
        for it in range(log2c - 1):
            q_bd = bf(_expand_blocks(q, C, G))
            if it < log2c - 2:
                pq = dot(bf(jnp.concatenate([p, q], axis=0)), q_bd, _NN)
                yield
                p = p + pq[:C]
                q = pq[C:]
            else:
                pq = dot(bf(p), q_bd, _NN)
                yield
                p = p + pq
        u = dot(bf(p), bf(_expand_blocks(rhs, RWKV_HEAD, G)), _NN)
        y_v = dot(m_rk, v_bd, _NN)
        s_v = dot(bf(vw), k_h, _TN)
        yield
        y_u = dot(m_rb, bf(_expand_blocks(u, RWKV_HEAD, G)), _NN)
        s_u = dot(bf(u), b_h, _TN)
        yield
        y = xs[C:] + y_u + y_v
        s_scr[i * gpb + gg] = s_old * jnp.exp(tot) + _expand_blocks_mask(s_u + s_v)
        inv_n = 1.0 / RWKV_HEAD
        mean = seg_sum(y) * inv_n
        yield
        d = y - mean
        var = seg_sum(d * d) * inv_n
        yield
        yn = d * lax.rsqrt(var + GN_EPS_RWKV) * gng_ref[:, cols] + gnb_ref[:, cols]
        yn = yn + bonus * vw
        gate = pg_ref[i, :, cols]
        o_ref[i, :, cols] = (yn * (gate * _sigmoid(gate))).astype(BF16)

    _round_robin(chain(i, gg) for i in range(nb) for gg in range(gpb))

    @pl.when(c == pl.num_programs(2) - 1)
    def _():
        for i in range(nb):
            for hg in range(gpb * G):
                lo = (hg % G) * RWKV_HEAD
                sout_ref[i, hg] = s_scr[i * gpb + hg // G, lo:lo + RWKV_HEAD, lo:lo + RWKV_HEAD]


def _expand_blocks_mask(s):
    lo = lax.broadcasted_iota(jnp.int32, (RWKV_HEAD, 128), 1) < RWKV_HEAD
    z = jnp.zeros((RWKV_HEAD, 128), F32)
    out = []
    for g in range(RWKV_GROUP):
        t = s[g * RWKV_HEAD:(g + 1) * RWKV_HEAD, (g // 2) * 128:(g // 2 + 1) * 128]
        t = jnp.where(lo, t, 0.0) if g % 2 == 0 else jnp.where(lo, 0.0, t)
        out.append(jnp.concatenate([t, z] if g < 2 else [z, t], axis=1))
    return jnp.concatenate(out, axis=0)


def _dot_exact_rhs_lhs(tri, x):
    xh, xl = _split(x)
    return jnp.dot(tri, xh, preferred_element_type=F32) + jnp.dot(tri, xl, preferred_element_type=F32)


def _rwkv(p0, plora, B, T, C, nb, gpb, pfirst, pfirst_lora, s0, prm):
    nc = T // C
    ng = RWKV_HEADS // (RWKV_GROUP * gpb)
    gw = GROUP_W
    bw = gpb * gw
    col = lambda off: pl.BlockSpec((nb, C, bw), lambda b, g, c: (b, c, off // bw + g))
    vec = lambda off: pl.BlockSpec((1, bw), lambda b, g, c: (0, off // bw + g))
    in_specs = [
        col(RKV_START), col(RKV_START + RWKV_W), col(RKV_START + 2 * RWKV_W), col(RKV_END),
        pl.BlockSpec((nb, C, 4 * LORA), lambda b, g, c: (b, c, 0)),
        vec(0), vec(RWKV_W), vec(2 * RWKV_W),
        vec(0), vec(0), vec(0), vec(0), vec(0), vec(0), vec(0),
        pl.BlockSpec((LORA, bw), lambda b, g, c: (0, g)),
        pl.BlockSpec((LORA, bw), lambda b, g, c: (0, g)),
        pl.BlockSpec((C, C), lambda b, g, c: (0, 0)),
        pl.BlockSpec((gw, gw), lambda b, g, c: (0, 0)),
    ]
    tri = jnp.asarray(np.tril(np.ones((C, C), np.float32)), BF16)
    seg = np.arange(gw) // RWKV_HEAD
    ones_bd = jnp.asarray((seg[:, None] == seg[None, :]).astype(np.float32), BF16)
    mu = prm["mu_rkv"].reshape(1, -1)
    args = [p0, p0, p0, p0, plora, mu, mu, mu, prm["w0"], prm["a0"], prm["k_k"], prm["k_a"], prm["r_k"],
            prm["gn_g"], prm["gn_b"], prm["w2"], prm["a2"], tri, ones_bd]
    state_spec = pl.BlockSpec((nb, gpb * RWKV_GROUP, RWKV_HEAD, RWKV_HEAD), lambda b, g, c: (b, g, 0, 0))
    has_state = s0 is not None
    if has_state:
        fcol = lambda off: pl.BlockSpec((nb, bw), lambda b, g, c: (b, off // bw + g))
        in_specs += [fcol(RKV_START), fcol(RKV_START + RWKV_W), fcol(RKV_START + 2 * RWKV_W),
                     pl.BlockSpec((nb, 4 * LORA), lambda b, g, c: (b, 0)), state_spec]
        args += [pfirst, pfirst, pfirst, pfirst_lora, s0]
    return pl.pallas_call(
        functools.partial(_rwkv_kernel, nb=nb, C=C, gpb=gpb, has_state=has_state),
        grid=(B // nb, ng, nc),
        in_specs=in_specs,
        out_specs=[pl.BlockSpec((nb, C, bw), lambda b, g, c: (b, c, g)), state_spec],
        out_shape=[jax.ShapeDtypeStruct((B, T, RWKV_W), BF16),
                   jax.ShapeDtypeStruct((B, RWKV_HEADS, RWKV_HEAD, RWKV_HEAD), F32)],
        scratch_shapes=[pltpu.VMEM((nb * gpb, gw, gw), F32), pltpu.VMEM((3, nb, 1, bw), F32),
                        pltpu.VMEM((nb, 1, 4 * LORA), F32)],
        compiler_params=_cparams(("parallel", "parallel", "arbitrary")),
        name="rwkv7",
    )(*args)


def _outproj0_kernel(x_ref, a1_ref, a2_ref, w1_ref, w2_ref, o_ref):
    acc = jnp.dot(a1_ref[...], w1_ref[...], preferred_element_type=F32)
    acc = acc + jnp.dot(a2_ref[...], w2_ref[...], preferred_element_type=F32)
    o_ref[...] = x_ref[...] + acc


def _outproj0(x, a1, a2, w, *, tm, tn):
    m, d = x.shape
    k1 = a1.shape[1]
    k2 = a2.shape[1]
    return pl.pallas_call(
        _outproj0_kernel,
        grid=(m // tm, d // tn),
        in_specs=[
            pl.BlockSpec((tm, tn), lambda i, j: (i, j)),
            pl.BlockSpec((tm, k1), lambda i, j: (i, 0)),
            pl.BlockSpec((tm, k2), lambda i, j: (i, 0)),
            pl.BlockSpec((k1, tn), lambda i, j: (0, j)),
            pl.BlockSpec((k2, tn), lambda i, j: (k1 // k2, j)),
        ],
        out_specs=pl.BlockSpec((tm, tn), lambda i, j: (i, j)),
        out_shape=jax.ShapeDtypeStruct((m, d), F32),
        compiler_params=_cparams(("parallel", "arbitrary")),
        name="outproj0",
    )(x, a1, a2, w, w)


def _outproj1_kernel(x_ref, y_ref, w_ref, g_ref, o_ref):
    out = x_ref[...] + jnp.dot(y_ref[...], w_ref[...], preferred_element_type=F32)
    o_ref[...] = out * lax.rsqrt(jnp.mean(out * out, axis=-1, keepdims=True) + NORM_EPS) * g_ref[...]


def _outproj1_norm(x, y, w, g, *, tm):
    m, d = x.shape
    k = y.shape[1]
    return pl.pallas_call(
        _outproj1_kernel,
        grid=(m // tm,),
        in_specs=[
            pl.BlockSpec((tm, d), lambda i: (i, 0)),
            pl.BlockSpec((tm, k), lambda i: (i, 0)),
            pl.BlockSpec((k, d), lambda i: (0, 0)),
            pl.BlockSpec((1, d), lambda i: (0, 0)),
        ],
        out_specs=pl.BlockSpec((tm, d), lambda i: (i, 0)),
        out_shape=jax.ShapeDtypeStruct((m, d), F32),
        compiler_params=_cparams(("parallel",)),
        name="outproj1_norm",
    )(x, y, w, g.reshape(1, d))


def _lru_gates(xc, wa_ref, wx_ref, ba, bx, spl):
    xcb = xc.astype(BF16)
    ra, rx = [], []
    for n in range(LRU_BLOCKS):
        blk = xcb[:, n * LRU_BW:(n + 1) * LRU_BW]
        ra.append(jnp.dot(blk, wa_ref[n], preferred_element_type=F32))
        rx.append(jnp.dot(blk, wx_ref[n], preferred_element_type=F32))
    gate_r = _sigmoid(jnp.concatenate(ra, axis=1) + ba)
    gate_i = _sigmoid(jnp.concatenate(rx, axis=1) + bx)
    log_a = -LRU_C * gate_r * spl
    a = jnp.exp(log_a)
    mult = jnp.sqrt(-jnp.tanh(log_a) * (a * a + 1.0))
    return a, mult, gate_i * xc


def _lru_prompt_kernel(xb_ref, gb_ref, cw_ref, cb_ref, wa_ref, wx_ref, ba_ref, bx_ref, lam_ref,
                       y_ref, hl_ref, xp_scr, a_scr, b_scr, h_scr, *, tt):
    i = pl.program_id(1)
    pad = 8

    @pl.when(i == 0)
    def _():
        xp_scr[0:pad, :] = jnp.zeros((pad, LRU_W), F32)
        h_scr[...] = jnp.zeros(h_scr.shape, F32)

    xp_scr[pad:pad + tt, :] = xb_ref[...]
    xc = cb_ref[...] + xp_scr[pad - 3:pad - 3 + tt, :] * cw_ref[0:1, :]
    for j in range(1, CONV_W):
        xc = xc + xp_scr[pad - 3 + j:pad - 3 + j + tt, :] * cw_ref[j:j + 1, :]
    xp_scr[0:pad, :] = xp_scr[tt:tt + pad, :]

    spl = _softplus(-lam_ref[...])
    a, mult, gx = _lru_gates(xc, wa_ref, wx_ref, ba_ref[...], bx_ref[...], spl)
    row = lax.broadcasted_iota(jnp.int32, (tt, LRU_W), 0)
    mult = jnp.where((row == 0) & (i == 0), 1.0, mult)
    a_scr[...] = a
    b_scr[...] = mult * gx

    def body(s, h):
        r0 = pl.multiple_of(s * 8, 8)
        a8 = a_scr[pl.ds(r0, 8), :]
        b8 = b_scr[pl.ds(r0, 8), :]
        out = []
        for j in range(8):
            h = a8[j:j + 1, :] * h + b8[j:j + 1, :]
            out.append(h)
        b_scr[pl.ds(r0, 8), :] = jnp.concatenate(out, axis=0)
        return h

    h = lax.fori_loop(0, tt // 8, body, h_scr[...], unroll=4)
    h_scr[...] = h
    gb = gb_ref[...]
    y_ref[...] = (b_scr[...] * (gb * _sigmoid(gb))).astype(BF16)

    @pl.when(i == pl.num_programs(1) - 1)
    def _():
        hl_ref[0] = h


def _lru_prompt(p1, B, T, prm, *, tt):
    nt = T // tt
    vec = pl.BlockSpec((1, LRU_W), lambda b, i: (0, 0))
    wspec = pl.BlockSpec((LRU_BLOCKS, LRU_BW, LRU_BW), lambda b, i: (0, 0, 0))
    return pl.pallas_call(
        functools.partial(_lru_prompt_kernel, tt=tt),
        grid=(B, nt),
        in_specs=[
            pl.BlockSpec((tt, LRU_W), lambda b, i: (b * nt + i, 0)),
            pl.BlockSpec((tt, LRU_W), lambda b, i: (b * nt + i, 1)),
            pl.BlockSpec((CONV_W, LRU_W), lambda b, i: (0, 0)),
            vec, wspec, wspec, vec, vec, vec,
        ],
        out_specs=[pl.BlockSpec((tt, LRU_W), lambda b, i: (b * nt + i, 0)),
                   pl.BlockSpec((1, 1, LRU_W), lambda b, i: (b, 0, 0))],
        out_shape=[jax.ShapeDtypeStruct((B * T, LRU_W), BF16), jax.ShapeDtypeStruct((B, 1, LRU_W), F32)],
        scratch_shapes=[pltpu.VMEM((tt + 8, LRU_W), F32), pltpu.VMEM((tt, LRU_W), F32),
                        pltpu.VMEM((tt, LRU_W), F32), pltpu.VMEM((1, LRU_W), F32)],
        compiler_params=_cparams(("parallel", "arbitrary")),
        name="lru_prompt",
    )(p1, p1, prm["conv_w"], prm["conv_b"], prm["wa"], prm["wx"], prm["ba"], prm["bx"], prm["lam"])


def _lru_sample_kernel(xb_ref, gb_ref, cbuf_ref, h0_ref, cw_ref, cb_ref, wa_ref, wx_ref, ba_ref, bx_ref, lam_ref,
                       y_ref, hl_ref, *, T):
    spl = _softplus(-lam_ref[...])
    h = h0_ref[...]
    for t in range(T):
        xc = cb_ref[...]
        for j in range(CONV_W):
            src = t + j - (CONV_W - 1)
            tap = xb_ref[src] if src >= 0 else cbuf_ref[src + CONV_W - 1]
            xc = xc + tap * cw_ref[j:j + 1, :]
        a, mult, gx = _lru_gates(xc, wa_ref, wx_ref, ba_ref[...], bx_ref[...], spl)
        h = a * h + mult * gx
        gb = gb_ref[t]
        y_ref[t] = (h * (gb * _sigmoid(gb))).astype(BF16)
    hl_ref[...] = h


def _lru_sample(p1_tm, cbuf_tm, h0, prm, *, bt):
    T, B, _ = p1_tm.shape
    vec = pl.BlockSpec((1, LRU_W), lambda i: (0, 0))
    wspec = pl.BlockSpec((LRU_BLOCKS, LRU_BW, LRU_BW), lambda i: (0, 0, 0))
    return pl.pallas_call(
        functools.partial(_lru_sample_kernel, T=T),
        grid=(B // bt,),
        in_specs=[
            pl.BlockSpec((T, bt, LRU_W), lambda i: (0, i, 0)),
            pl.BlockSpec((T, bt, LRU_W), lambda i: (0, i, 1)),
            pl.BlockSpec((CONV_W - 1, bt, LRU_W), lambda i: (0, i, 0)),
            pl.BlockSpec((bt, LRU_W), lambda i: (i, 0)),
            pl.BlockSpec((CONV_W, LRU_W), lambda i: (0, 0)),
            vec, wspec, wspec, vec, vec, vec,
        ],
        out_specs=[pl.BlockSpec((T, bt, LRU_W), lambda i: (0, i, 0)), pl.BlockSpec((bt, LRU_W), lambda i: (i, 0))],
        out_shape=[jax.ShapeDtypeStruct((T, B, LRU_W), BF16), jax.ShapeDtypeStruct((B, LRU_W), F32)],
        compiler_params=_cparams(("parallel",)),
        name="lru_sample",
    )(p1_tm, p1_tm, cbuf_tm, h0, prm["conv_w"], prm["conv_b"], prm["wa"], prm["wx"], prm["ba"], prm["bx"], prm["lam"])


def kernel(x_prompt, x_sample, state_ret, state_rwkv_wkv, state_rwkv_shift, state_lru_h, state_lru_conv,
           norm0_g, w_in0, w_out0, rwkv_mu_rkv, rwkv_mu_w, rwkv_mu_a, rwkv_w0, rwkv_w1, rwkv_w2,
           rwkv_a0, rwkv_a1, rwkv_a2, rwkv_k_k, rwkv_k_a, rwkv_r_k, rwkv_gn_g, rwkv_gn_b,
           norm1_g, w_in1, lru_conv_w, lru_conv_b, lru_wa, lru_ba, lru_wx, lru_bx, lru_lambda, w_out1,
           final_g):
    bp, tp, d = x_prompt.shape
    bs, ts, _ = x_sample.shape
    xp = x_prompt.reshape(bp * tp, d)
    xs = x_sample.reshape(bs * ts, d)

    mw = rwkv_mu_w[:, None]
    ma = rwkv_mu_a[:, None]
    w_lora = jnp.concatenate([(1.0 - mw) * rwkv_w1, mw * rwkv_w1, (1.0 - ma) * rwkv_a1, ma * rwkv_a1],
                             axis=1).astype(BF16)
    w_in0_b = w_in0.astype(BF16)
    w_out0_b = w_out0.astype(BF16)
    w_in1_b = w_in1.astype(BF16)
    w_out1_b = w_out1.astype(BF16)
    row = lambda v: v.reshape(1, -1)
    rwkv_prm = dict(mu_rkv=rwkv_mu_rkv, w0=row(rwkv_w0), a0=row(rwkv_a0), k_k=row(rwkv_k_k), k_a=row(rwkv_k_a),
                    r_k=row(rwkv_r_k), gn_g=row(rwkv_gn_g), gn_b=row(rwkv_gn_b),
                    w2=rwkv_w2.astype(BF16), a2=rwkv_a2.astype(BF16))
    lru_prm = dict(conv_w=lru_conv_w, conv_b=row(lru_conv_b), wa=lru_wa.astype(BF16), wx=lru_wx.astype(BF16),
                   ba=row(lru_ba), bx=row(lru_bx), lam=row(lru_lambda))

    p0p, plp = _norm_matmul(xp, norm0_g, w_in0_b, w_lora, apply_norm=True, tm=_tile(bp * tp, 1024), tn=1024)
    p0s, pls = _norm_matmul(xs, norm0_g, w_in0_b, w_lora, apply_norm=True, tm=bs * ts, tn=1024)
    pfirst, pfirst_lora = _norm_matmul(state_rwkv_shift, norm0_g, w_in0_b, w_lora, apply_norm=False, tm=bs, tn=512)
    p0p = p0p.reshape(bp, tp, IN0)
    p0s = p0s.reshape(bs, ts, IN0)

    cp = RET_CHUNK if tp % RET_CHUNK == 0 else tp
    cs = RET_CHUNK if ts % RET_CHUNK == 0 else ts
    nbp = 2 if bp % 2 == 0 else 1
    ret_p, new_ret_p = _retention(p0p, bp, tp, cp, nbp, nbp, _ret_tables(np.arange(tp), cp), None)
    ret_s, new_ret_s = _retention(p0s, bs, ts, cs, 8, 4, _ret_tables(PAST_LEN + np.arange(ts), cs), state_ret)

    cwp = RWKV_CHUNK if tp % RWKV_CHUNK == 0 else tp
    rw_p, wkv_p = _rwkv(p0p, plp.reshape(bp, tp, 4 * LORA), bp, tp, cwp, bp, 4, None, None, None, rwkv_prm)
    rw_s, wkv_s = _rwkv(p0s, pls.reshape(bs, ts, 4 * LORA), bs, ts, ts, 8, 2, pfirst, pfirst_lora, state_rwkv_wkv,
                        rwkv_prm)
    ret_p = ret_p.reshape(bp * tp, RET_V)
    ret_s = ret_s.reshape(bs * ts, RET_V)
    rw_p = rw_p.reshape(bp * tp, RWKV_W)
    rw_s = rw_s.reshape(bs * ts, RWKV_W)

    x1p = _outproj0(xp, ret_p, rw_p, w_out0_b, tm=_tile(bp * tp, 1024), tn=1024)
    x1s = _outproj0(xs, ret_s, rw_s, w_out0_b, tm=512, tn=1024)

    last_rows = jnp.concatenate([x_prompt[:, -1, :], x_sample[:, -1, :]], axis=0)
    shift = _rmsnorm_rows(last_rows, norm0_g)

    p1p = _norm_matmul(x1p, norm1_g, w_in1_b, apply_norm=True, tm=_tile(bp * tp, 1024), tn=1024)
    x1s_tm = x1s.reshape(bs, ts, d).transpose(1, 0, 2).reshape(ts * bs, d)
    p1s = _norm_matmul(x1s_tm, norm1_g, w_in1_b, apply_norm=True, tm=512, tn=512)

    yp, lh_p = _lru_prompt(p1p, bp, tp, lru_prm, tt=256)
    ys_tm, lh_s = _lru_sample(p1s.reshape(ts, bs, 2 * LRU_W), state_lru_conv.transpose(1, 0, 2), state_lru_h,
                              lru_prm, bt=64)

    out_p = _outproj1_norm(x1p, yp, w_out1_b, final_g, tm=512)
    out_s_tm = _outproj1_norm(x1s_tm, ys_tm.reshape(ts * bs, LRU_W), w_out1_b, final_g, tm=256)

    y_prompt = out_p.reshape(bp, tp, d)
    y_sample = out_s_tm.reshape(ts, bs, d).transpose(1, 0, 2)
    lc_p = p1p.reshape(bp, tp, 2 * LRU_W)[:, tp - (CONV_W - 1):, :LRU_W]
    lc_s = p1s.reshape(ts, bs, 2 * LRU_W)[ts - (CONV_W - 1):, :, :LRU_W].transpose(1, 0, 2)
    return (y_prompt, y_sample,
            new_ret_p, wkv_p, shift[:bp], lh_p.reshape(bp, LRU_W), lc_p,
            new_ret_s, wkv_s, shift[bp:], lh_s, lc_s)
```

```python
import functools

import numpy as np
import jax
import jax.numpy as jnp
from jax import lax
from jax.experimental import pallas as pl
from jax.experimental.pallas import tpu as pltpu

F32 = jnp.float32
BF16 = jnp.bfloat16

D_MODEL = 2048
PAST_LEN = 16384
RET_HEADS = 8
RET_DK = 64
RET_DV = 128
RET_QK = RET_HEADS * RET_DK
RET_V = RET_HEADS * RET_DV
RET_CHUNK = 128
ROPE_BASE = 10000.0
RWKV_HEAD = 64
RWKV_HEADS = 16
RWKV_W = RWKV_HEADS * RWKV_HEAD
LORA = 64
GN_EPS_RWKV = 64e-5
LRU_W = 2048
LRU_BLOCKS = 8
LRU_BW = LRU_W // LRU_BLOCKS
CONV_W = 4
LRU_C = 8.0
NORM_EPS = 1e-6
RKV_START = 2 * RET_QK + 2 * RET_V
RKV_END = RKV_START + 3 * RWKV_W
IN0 = RKV_END + RWKV_W

RWKV_GROUP = 4
GROUP_W = RWKV_GROUP * RWKV_HEAD
RWKV_CHUNK = 64

VMEM_LIMIT = 48 * 1024 * 1024


def _tile(m, pref):
    return pref if m % pref == 0 else m


def _round_robin(gens):
    gens = list(gens)
    while gens:
        alive = []
        for g in gens:
            try:
                next(g)
                alive.append(g)
            except StopIteration:
                pass
        gens = alive


def _cparams(sem):
    return pltpu.CompilerParams(dimension_semantics=sem, vmem_limit_bytes=VMEM_LIMIT)


def _split(x):
    hi = x.astype(BF16)
    lo = (x - hi.astype(F32)).astype(BF16)
    return hi, lo


_NN = (((1,), (0,)), ((), ()))
_NT = (((1,), (1,)), ((), ()))
_TN = (((0,), (0,)), ((), ()))


def _dot1(a, b, dims=_NN):
    return lax.dot_general(a.astype(BF16), b.astype(BF16), dims, preferred_element_type=F32)


def _dot_exact_rhs(a, b, dims=_NN):
    ah, al = _split(a)
    d = functools.partial(lax.dot_general, dimension_numbers=dims, preferred_element_type=F32)
    return d(ah, b) + d(al, b)


def _softplus(x):
    return jnp.maximum(x, 0.0) + jnp.log1p(jnp.exp(-jnp.abs(x)))


def _sigmoid(x):
    return 1.0 / (1.0 + jnp.exp(-x))


def _norm_matmul_kernel(x_ref, g_ref, w_ref, *rest, apply_norm, has_extra):
    if has_extra:
        we_ref, o_ref, oe_ref, h_ref = rest
    else:
        o_ref, h_ref = rest

    @pl.when(pl.program_id(1) == 0)
    def _():
        x = x_ref[...]
        if apply_norm:
            x = x * lax.rsqrt(jnp.mean(x * x, axis=-1, keepdims=True) + NORM_EPS) * g_ref[...]
        h_ref[...] = x.astype(BF16)
        if has_extra:
            oe_ref[...] = jnp.dot(h_ref[...], we_ref[...], preferred_element_type=F32)

    o_ref[...] = jnp.dot(h_ref[...], w_ref[...], preferred_element_type=F32)


def _norm_matmul(x, g, w, w_extra=None, *, apply_norm, tm, tn):
    m, d = x.shape
    n = w.shape[1]
    has_extra = w_extra is not None
    in_specs = [
        pl.BlockSpec((tm, d), lambda i, j: (i, 0)),
        pl.BlockSpec((1, d), lambda i, j: (0, 0)),
        pl.BlockSpec((d, tn), lambda i, j: (0, j)),
    ]
    out_specs = [pl.BlockSpec((tm, tn), lambda i, j: (i, j))]
    out_shape = [jax.ShapeDtypeStruct((m, n), F32)]
    args = [x, g.reshape(1, d), w]
    if has_extra:
        ne = w_extra.shape[1]
        in_specs.append(pl.BlockSpec((d, ne), lambda i, j: (0, 0)))
        out_specs.append(pl.BlockSpec((tm, ne), lambda i, j: (i, 0)))
        out_shape.append(jax.ShapeDtypeStruct((m, ne), F32))
        args.append(w_extra)
    out = pl.pallas_call(
        functools.partial(_norm_matmul_kernel, apply_norm=apply_norm, has_extra=has_extra),
        grid=(m // tm, n // tn),
        in_specs=in_specs,
        out_specs=out_specs,
        out_shape=out_shape,
        scratch_shapes=[pltpu.VMEM((tm, d), BF16)],
        compiler_params=_cparams(("parallel", "arbitrary")),
        name="norm_matmul",
    )(*args)
    return out if has_extra else out[0]


def _rmsnorm_kernel(x_ref, g_ref, o_ref):
    x = x_ref[...]
    o_ref[...] = x * lax.rsqrt(jnp.mean(x * x, axis=-1, keepdims=True) + NORM_EPS) * g_ref[...]


def _rmsnorm_rows(x, g):
    return pl.pallas_call(
        _rmsnorm_kernel,
        out_shape=jax.ShapeDtypeStruct(x.shape, F32),
        name="rmsnorm_rows",
    )(x, g.reshape(1, -1))


def _ret_kernel(q_ref, k_ref, v_ref, g_ref, cos_ref, sin_ref, dmask_ref, qdec_ref, kdec_ref, sdec_ref,
                *rest, nb, C, unroll, has_s0):
    if has_s0:
        s0_ref, o_ref, sout_ref, s_scr = rest
    else:
        o_ref, sout_ref, s_scr = rest
    c = pl.program_id(1)

    @pl.when(c == 0)
    def _():
        if has_s0:
            s_scr[...] = s0_ref[...]
        else:
            s_scr[...] = jnp.zeros(s_scr.shape, F32)

    pw = 2 * RET_DK
    lane = lax.broadcasted_iota(jnp.int32, (C, pw), 1)
    first_half = (lane & (RET_DK // 2)) == 0
    cos = cos_ref[...]
    sin = sin_ref[...]

    def rope(x):
        rot = jnp.where(first_half, pltpu.roll(x, pw - RET_DK // 2, 1), pltpu.roll(x, RET_DK // 2, 1))
        return x * cos + rot * sin

    def chain(i, hp):
        q = rope(q_ref[i, :, hp * pw:(hp + 1) * pw])
        k = rope(k_ref[i, :, hp * pw:(hp + 1) * pw]) * (RET_DK ** -0.5)
        kd = k * kdec_ref[hp]
        part = []
        for h2 in range(2):
            h = 2 * hp + h2
            qh = q[:, h2 * RET_DK:(h2 + 1) * RET_DK].astype(BF16)
            kh = k[:, h2 * RET_DK:(h2 + 1) * RET_DK].astype(BF16)
            kdh = kd[:, h2 * RET_DK:(h2 + 1) * RET_DK].astype(BF16)
            vh = v_ref[i, :, h * RET_DV:(h + 1) * RET_DV].astype(BF16)
            s_old = s_scr[i, h]
            sc = lax.dot_general(qh, kh, _NT, preferred_element_type=F32)
            cross = jnp.dot(qh, s_old.astype(BF16), preferred_element_type=F32)
            s_add = lax.dot_general(kdh, vh, _TN, preferred_element_type=F32)
            part.append((h, vh, s_old, sc, cross, s_add))
        yield
        outs = []
        for h, vh, s_old, sc, cross, s_add in part:
            s_scr[i, h] = s_old * sdec_ref[h] + s_add
            o = jnp.dot((sc * dmask_ref[h]).astype(BF16), vh, preferred_element_type=F32)
            outs.append((h, o, cross))
        yield
        for h, o, cross in outs:
            o = o + cross * qdec_ref[h]
            o = o * lax.rsqrt(jnp.mean(o * o, axis=-1, keepdims=True) + NORM_EPS)
            gate = g_ref[i, :, h * RET_DV:(h + 1) * RET_DV]
            o_ref[i, :, h * RET_DV:(h + 1) * RET_DV] = (o * (gate * _sigmoid(gate))).astype(BF16)

    if nb == unroll:
        _round_robin(chain(i, hp) for i in range(nb) for hp in range(RET_HEADS // 2))
    else:
        def body(j, carry):
            _round_robin(chain(j * unroll + u, hp) for u in range(unroll) for hp in range(RET_HEADS // 2))
            return carry
        lax.fori_loop(0, nb // unroll, body, 0)

    @pl.when(c == pl.num_programs(1) - 1)
    def _():
        sout_ref[...] = s_scr[...]


def _ret_tables(pos, C):
    inv = 1.0 / (ROPE_BASE ** (np.arange(0, RET_DK, 2, dtype=np.float64) / RET_DK))
    ang = pos.astype(np.float64)[:, None] * inv[None, :]
    cos = np.tile(np.cos(ang), (1, 4))
    sin = np.tile(np.concatenate([-np.sin(ang), np.sin(ang)], axis=1), (1, 2))
    log_gamma = np.log(1.0 - 2.0 ** (-5.0 - np.arange(RET_HEADS, dtype=np.float64)))
    n = np.arange(C, dtype=np.float64)
    diff = n[:, None] - n[None, :]
    dmask = np.where(diff >= 0, np.exp(log_gamma[:, None, None] * np.maximum(diff, 0.0)), 0.0)
    qdec = np.broadcast_to(np.exp(log_gamma[:, None] * (n + 1.0))[:, :, None], (RET_HEADS, C, RET_DV))
    kdec = np.exp(log_gamma[:, None] * (C - 1.0 - n))
    kdec = np.repeat(kdec.reshape(RET_HEADS // 2, 2, C).transpose(0, 2, 1), RET_DK, axis=2)
    sdec = np.broadcast_to(np.exp(log_gamma * C)[:, None, None], (RET_HEADS, 1, RET_DV))
    f = lambda a: jnp.asarray(np.ascontiguousarray(a), F32)
    return f(cos), f(sin), f(dmask), f(qdec), f(kdec), f(sdec)


def _retention(p0, B, T, C, nb, unroll, tables, s0):
    cos, sin, dmask, qdec, kdec, sdec = tables
    nc = T // C
    whole = lambda a: pl.BlockSpec(a.shape, lambda b, c: (0,) * a.ndim)
    in_specs = [
        pl.BlockSpec((nb, C, RET_QK), lambda b, c: (b, c, 0)),
        pl.BlockSpec((nb, C, RET_QK), lambda b, c: (b, c, 1)),
        pl.BlockSpec((nb, C, RET_V), lambda b, c: (b, c, 2 * RET_QK // RET_V)),
        pl.BlockSpec((nb, C, RET_V), lambda b, c: (b, c, 2 * RET_QK // RET_V + 1)),
        pl.BlockSpec((C, 2 * RET_DK), lambda b, c: (c, 0)),
        pl.BlockSpec((C, 2 * RET_DK), lambda b, c: (c, 0)),
        whole(dmask), whole(qdec), whole(kdec), whole(sdec),
    ]
    args = [p0, p0, p0, p0, cos, sin, dmask, qdec, kdec, sdec]
    state_spec = pl.BlockSpec((nb, RET_HEADS, RET_DK, RET_DV), lambda b, c: (b, 0, 0, 0))
    if s0 is not None:
        in_specs.append(state_spec)
        args.append(s0)
    return pl.pallas_call(
        functools.partial(_ret_kernel, nb=nb, C=C, unroll=unroll, has_s0=s0 is not None),
        grid=(B // nb, nc),
        in_specs=in_specs,
        out_specs=[pl.BlockSpec((nb, C, RET_V), lambda b, c: (b, c, 0)), state_spec],
        out_shape=[jax.ShapeDtypeStruct((B, T, RET_V), BF16),
                   jax.ShapeDtypeStruct((B, RET_HEADS, RET_DK, RET_DV), F32)],
        scratch_shapes=[pltpu.VMEM((nb, RET_HEADS, RET_DK, RET_DV), F32)],
        compiler_params=_cparams(("parallel", "arbitrary")),
        name="retention",
    )(*args)


def _expand_blocks(y, blk, groups):
    rows, w = y.shape
    if blk == RWKV_HEAD and w == 2 * 128 and groups == 4:
        lo = lax.broadcasted_iota(jnp.int32, (rows, 128), 1) < blk
        z = jnp.zeros((rows, 128), F32)
        y0, y1 = y[:, :128], y[:, 128:]
        return jnp.concatenate([
            jnp.concatenate([jnp.where(lo, y0, 0.0), z], axis=1),
            jnp.concatenate([jnp.where(lo, 0.0, y0), z], axis=1),
            jnp.concatenate([z, jnp.where(lo, y1, 0.0)], axis=1),
            jnp.concatenate([z, jnp.where(lo, 0.0, y1)], axis=1)], axis=0)
    shift_r = rows.bit_length() - 1
    shift_l = blk.bit_length() - 1
    mask = (lax.broadcasted_iota(jnp.int32, (groups * rows, w), 0) >> shift_r) == \
           (lax.broadcasted_iota(jnp.int32, (groups * rows, w), 1) >> shift_l)
    return jnp.where(mask, jnp.concatenate([y] * groups, axis=0), 0.0)


def _rwkv_kernel(pr_ref, pk_ref, pv_ref, pg_ref, pl_ref, mur_ref, muk_ref, muv_ref, w0_ref, a0_ref, kk_ref, ka_ref,
                 rk_ref, gng_ref, gnb_ref, w2_ref, a2_ref, tri_ref, ones_ref, *rest, nb, C, gpb, has_state):
    if has_state:
        fr_ref, fk_ref, fv_ref, fl_ref, s0_ref, o_ref, sout_ref, s_scr, prev_scr, prevl_scr = rest
    else:
        o_ref, sout_ref, s_scr, prev_scr, prevl_scr = rest
    c = pl.program_id(2)
    G = RWKV_GROUP
    GC = G * C
    log2c = C.bit_length() - 1
    pair_w = 2 * RWKV_HEAD
    bf = lambda x: x.astype(BF16)
    dot = functools.partial(lax.dot_general, preferred_element_type=F32)

    @pl.when(c == 0)
    def _():
        s_scr[...] = jnp.zeros(s_scr.shape, F32)
        prev_scr[...] = jnp.zeros(prev_scr.shape, F32)
        prevl_scr[...] = jnp.zeros(prevl_scr.shape, F32)
        if has_state:
            z = jnp.zeros((RWKV_HEAD, RWKV_HEAD), F32)
            for i in range(nb):
                for hg in range(gpb * G):
                    g = hg % G
                    head = s0_ref[i, hg]
                    blk = jnp.concatenate([head, z] if g % 2 == 0 else [z, head], axis=1)
                    s_scr[i * gpb + hg // G, g * RWKV_HEAD:(g + 1) * RWKV_HEAD,
                          (g // 2) * pair_w:(g // 2 + 1) * pair_w] = blk

    is_row0 = lax.broadcasted_iota(jnp.int32, (C, GROUP_W), 0) == 0
    t_idx = lax.broadcasted_iota(jnp.int32, (C, GC), 0)
    s_idx = lax.broadcasted_iota(jnp.int32, (C, GC), 1) & (C - 1)
    strict = s_idx < t_idx
    incl = s_idx <= t_idx
    eye_lb = jnp.where(s_idx == t_idx, 1.0, 0.0).astype(F32)
    ones_bd = ones_ref[...]

    def seg_sum(x):
        return _dot1(x, ones_bd)

    def lora_inputs(i):
        plo = pl_ref[i]
        prev = fl_ref[pl.ds(i, 1), :] if has_state else prevl_scr[i]
        plo_s = jnp.where(is_row0, prev, pltpu.roll(plo, 1, 0))
        if not has_state:
            prevl_scr[i] = plo[C - 1:C, :]
        pre_w = plo[:, 0:LORA] + plo_s[:, LORA:2 * LORA]
        pre_a = plo[:, 2 * LORA:3 * LORA] + plo_s[:, 3 * LORA:4 * LORA]
        return bf(jnp.tanh(pre_w)), bf(pre_a)

    lora = [lora_inputs(i) for i in range(nb)]

    def chain(i, gg):
        cols = slice(gg * GROUP_W, (gg + 1) * GROUP_W)

        def shifted(x, k, first_ref):
            prev = first_ref[pl.ds(i, 1), cols] if has_state else prev_scr[k, i, :, cols]
            return jnp.where(is_row0, prev, pltpu.roll(x, 1, 0))

        pr = pr_ref[i, :, cols]
        pk = pk_ref[i, :, cols]
        pv = pv_ref[i, :, cols]
        pr_s = shifted(pr, 0, fr_ref if has_state else None)
        pk_s = shifted(pk, 1, fk_ref if has_state else None)
        pv_s = shifted(pv, 2, fv_ref if has_state else None)
        if not has_state:
            prev_scr[0, i, :, cols] = pr[C - 1:C, :]
            prev_scr[1, i, :, cols] = pk[C - 1:C, :]
            prev_scr[2, i, :, cols] = pv[C - 1:C, :]

        r = pr + mur_ref[:, cols] * (pr_s - pr)
        kw = pk + muk_ref[:, cols] * (pk_s - pk)
        vw = pv + muv_ref[:, cols] * (pv_s - pv)
        wl = dot(lora[i][0], w2_ref[:, cols], _NN)
        al = dot(lora[i][1], a2_ref[:, cols], _NN)
        kk = kw * kk_ref[:, cols]
        kk_ss = seg_sum(kk * kk)
        yield
        w_log = -_softplus(-(w0_ref[:, cols] + wl)) - 0.5
        logw = -jnp.exp(w_log)
        cum = _dot_exact_rhs_lhs(tri_ref[...], logw)
        a = _sigmoid(a0_ref[:, cols] + al)
        kk = kk / jnp.maximum(jnp.sqrt(kk_ss), 1e-12)
        k = kw * (1.0 + (a - 1.0) * ka_ref[:, cols])
        bv = kk * a
        bonus = seg_sum(r * k * rk_ref[:, cols])
        v_bd = bf(_expand_blocks(vw, RWKV_HEAD, G))
        yield
        tot = cum[C - 1:C, :]
        e_neg = jnp.exp(-cum)
        e_rem = jnp.exp(tot - cum)
        a_t = -kk * jnp.exp(cum - logw)
        r_t = r * jnp.exp(cum)
        b_t = bv * e_neg
        k_t = k * e_neg
        b_h = bf(bv * e_rem)
        k_h = bf(k * e_rem)

        s_old = s_scr[i * gpb + gg]
        ar = bf(jnp.concatenate([a_t, r_t], axis=0))
        xb = dot(ar, bf(_expand_blocks(b_t, RWKV_HEAD, G)), _NT)
        xk = dot(ar, bf(_expand_blocks(k_t, RWKV_HEAD, G)), _NT)
        xs = dot(ar, bf(s_old), _NT)
        yield
        l_ab = jnp.where(strict, xb[:C], 0.0)
        l_ak = jnp.where(strict, xk[:C], 0.0)
        q = dot(bf(l_ab), bf(_expand_blocks(l_ab, C, G)), _NN)
        rhs = xs[:C] + dot(bf(l_ak), v_bd, _NN)
        p = eye_lb + l_ab
        m_rb = bf(jnp.where(incl, xb[C:], 0.0))
        m_rk = bf(jnp.where(incl, xk[C:], 0.0))
        yield
        for it in range(log2c - 1):
            q_bd = bf(_expand_blocks(q, C, G))
            if it < log2c - 2:
                pq = dot(bf(jnp.concatenate([p, q], axis=0)), q_bd, _NN)
                yield
                p = p + pq[:C]
                q = pq[C:]
            else:
                pq = dot(bf(p), q_bd, _NN)
                yield
                p = p + pq
        u = dot(bf(p), bf(_expand_blocks(rhs, RWKV_HEAD, G)), _NN)
        y_v = dot(m_rk, v_bd, _NN)
        s_v = dot(bf(vw), k_h, _TN)
        yield
        y_u = dot(m_rb, bf(_expand_blocks(u, RWKV_HEAD, G)), _NN)
        s_u = dot(bf(u), b_h, _TN)
        yield
        y = xs[C:] + y_u + y_v
        s_scr[i * gpb + gg] = s_old * jnp.exp(tot) + _expand_blocks_mask(s_u + s_v)
        inv_n = 1.0 / RWKV_HEAD
        mean = seg_sum(y) * inv_n
        yield
        d = y - mean
        var = seg_sum(d * d) * inv_n
        yield
        yn = d * lax.rsqrt(var + GN_EPS_RWKV) * gng_ref[:, cols] + gnb_ref[:, cols]
        yn = yn + bonus * vw
        gate = pg_ref[i, :, cols]
        o_ref[i, :, cols] = (yn * (gate * _sigmoid(gate))).astype(BF16)

    _round_robin(chain(i, gg) for i in range(nb) for gg in range(gpb))

    @pl.when(c == pl.num_programs(2) - 1)
    def _():
        for i in range(nb):
            for hg in range(gpb * G):
                lo = (hg % G) * RWKV_HEAD
                sout_ref[i, hg] = s_scr[i * gpb + hg // G, lo:lo + RWKV_HEAD, lo:lo + RWKV_HEAD]


def _expand_blocks_mask(s):
    lo = lax.broadcasted_iota(jnp.int32, (RWKV_HEAD, 128), 1) < RWKV_HEAD
    z = jnp.zeros((RWKV_HEAD, 128), F32)
    out = []
    for g in range(RWKV_GROUP):
        t = s[g * RWKV_HEAD:(g + 1) * RWKV_HEAD, (g // 2) * 128:(g // 2 + 1) * 128]
        t = jnp.where(lo, t, 0.0) if g % 2 == 0 else jnp.where(lo, 0.0, t)
        out.append(jnp.concatenate([t, z] if g < 2 else [z, t], axis=1))
    return jnp.concatenate(out, axis=0)


def _dot_exact_rhs_lhs(tri, x):
    xh, xl = _split(x)
    return jnp.dot(tri, xh, preferred_element_type=F32) + jnp.dot(tri, xl, preferred_element_type=F32)


def _rwkv(p0, plora, B, T, C, nb, gpb, pfirst, pfirst_lora, s0, prm):
    nc = T // C
    ng = RWKV_HEADS // (RWKV_GROUP * gpb)
    gw = GROUP_W
    bw = gpb * gw
    col = lambda off: pl.BlockSpec((nb, C, bw), lambda b, g, c: (b, c, off // bw + g))
    vec = lambda off: pl.BlockSpec((1, bw), lambda b, g, c: (0, off // bw + g))
    in_specs = [
        col(RKV_START), col(RKV_START + RWKV_W), col(RKV_START + 2 * RWKV_W), col(RKV_END),
        pl.BlockSpec((nb, C, 4 * LORA), lambda b, g, c: (b, c, 0)),
        vec(0), vec(RWKV_W), vec(2 * RWKV_W),
        vec(0), vec(0), vec(0), vec(0), vec(0), vec(0), vec(0),
        pl.BlockSpec((LORA, bw), lambda b, g, c: (0, g)),
        pl.BlockSpec((LORA, bw), lambda b, g, c: (0, g)),
        pl.BlockSpec((C, C), lambda b, g, c: (0, 0)),
        pl.BlockSpec((gw, gw), lambda b, g, c: (0, 0)),
    ]
    tri = jnp.asarray(np.tril(np.ones((C, C), np.float32)), BF16)
    seg = np.arange(gw) // RWKV_HEAD
    ones_bd = jnp.asarray((seg[:, None] == seg[None, :]).astype(np.float32), BF16)
    mu = prm["mu_rkv"].reshape(1, -1)
    args = [p0, p0, p0, p0, plora, mu, mu, mu, prm["w0"], prm["a0"], prm["k_k"], prm["k_a"], prm["r_k"],
            prm["gn_g"], prm["gn_b"], prm["w2"], prm["a2"], tri, ones_bd]
    state_spec = pl.BlockSpec((nb, gpb * RWKV_GROUP, RWKV_HEAD, RWKV_HEAD), lambda b, g, c: (b, g, 0, 0))
    has_state = s0 is not None
    if has_state:
        fcol = lambda off: pl.BlockSpec((nb, bw), lambda b, g, c: (b, off // bw + g))
        in_specs += [fcol(RKV_START), fcol(RKV_START + RWKV_W), fcol(RKV_START + 2 * RWKV_W),
                     pl.BlockSpec((nb, 4 * LORA), lambda b, g, c: (b, 0)), state_spec]
        args += [pfirst, pfirst, pfirst, pfirst_lora, s0]
    return pl.pallas_call(
        functools.partial(_rwkv_kernel, nb=nb, C=C, gpb=gpb, has_state=has_state),
        grid=(B // nb, ng, nc),
        in_specs=in_specs,
        out_specs=[pl.BlockSpec((nb, C, bw), lambda b, g, c: (b, c, g)), state_spec],
        out_shape=[jax.ShapeDtypeStruct((B, T, RWKV_W), BF16),
                   jax.ShapeDtypeStruct((B, RWKV_HEADS, RWKV_HEAD, RWKV_HEAD), F32)],
        scratch_shapes=[pltpu.VMEM((nb * gpb, gw, gw), F32), pltpu.VMEM((3, nb, 1, bw), F32),
                        pltpu.VMEM((nb, 1, 4 * LORA), F32)],
        compiler_params=_cparams(("parallel", "parallel", "arbitrary")),
        name="rwkv7",
    )(*args)


def _outproj0_kernel(x_ref, a1_ref, a2_ref, w1_ref, w2_ref, o_ref):
    acc = jnp.dot(a1_ref[...], w1_ref[...], preferred_element_type=F32)
    acc = acc + jnp.dot(a2_ref[...], w2_ref[...], preferred_element_type=F32)
    o_ref[...] = x_ref[...] + acc


def _outproj0(x, a1, a2, w, *, tm, tn):
    m, d = x.shape
    k1 = a1.shape[1]
    k2 = a2.shape[1]
    return pl.pallas_call(
        _outproj0_kernel,
        grid=(m // tm, d // tn),
        in_specs=[
            pl.BlockSpec((tm, tn), lambda i, j: (i, j)),
            pl.BlockSpec((tm, k1), lambda i, j: (i, 0)),
            pl.BlockSpec((tm, k2), lambda i, j: (i, 0)),
            pl.BlockSpec((k1, tn), lambda i, j: (0, j)),
            pl.BlockSpec((k2, tn), lambda i, j: (k1 // k2, j)),
        ],
        out_specs=pl.BlockSpec((tm, tn), lambda i, j: (i, j)),
        out_shape=jax.ShapeDtypeStruct((m, d), F32),
        compiler_params=_cparams(("parallel", "arbitrary")),
        name="outproj0",
    )(x, a1, a2, w, w)


def _outproj1_kernel(x_ref, y_ref, w_ref, g_ref, o_ref):
    out = x_ref[...] + jnp.dot(y_ref[...], w_ref[...], preferred_element_type=F32)
    o_ref[...] = out * lax.rsqrt(jnp.mean(out * out, axis=-1, keepdims=True) + NORM_EPS) * g_ref[...]


def _outproj1_norm(x, y, w, g, *, tm):
    m, d = x.shape
    k = y.shape[1]
    return pl.pallas_call(
        _outproj1_kernel,
        grid=(m // tm,),
        in_specs=[
            pl.BlockSpec((tm, d), lambda i: (i, 0)),
            pl.BlockSpec((tm, k), lambda i: (i, 0)),
            pl.BlockSpec((k, d), lambda i: (0, 0)),
            pl.BlockSpec((1, d), lambda i: (0, 0)),
        ],
        out_specs=pl.BlockSpec((tm, d), lambda i: (i, 0)),
        out_shape=jax.ShapeDtypeStruct((m, d), F32),
        compiler_params=_cparams(("parallel",)),
        name="outproj1_norm",
    )(x, y, w, g.reshape(1, d))


def _lru_gates(xc, wa_ref, wx_ref, ba, bx, spl):
    xcb = xc.astype(BF16)
    ra, rx = [], []
    for n in range(LRU_BLOCKS):
        blk = xcb[:, n * LRU_BW:(n + 1) * LRU_BW]
        ra.append(jnp.dot(blk, wa_ref[n], preferred_element_type=F32))
        rx.append(jnp.dot(blk, wx_ref[n], preferred_element_type=F32))
    gate_r = _sigmoid(jnp.concatenate(ra, axis=1) + ba)
    gate_i = _sigmoid(jnp.concatenate(rx, axis=1) + bx)
    log_a = -LRU_C * gate_r * spl
    a = jnp.exp(log_a)
    mult = jnp.sqrt(-jnp.tanh(log_a) * (a * a + 1.0))
    return a, mult, gate_i * xc


def _lru_prompt_kernel(xb_ref, gb_ref, cw_ref, cb_ref, wa_ref, wx_ref, ba_ref, bx_ref, lam_ref,
                       y_ref, hl_ref, xp_scr, a_scr, b_scr, h_scr, *, tt, nb):
    i = pl.program_id(0)
    pad = 8

    @pl.when(i == 0)
    def _():
        xp_scr[:, 0:pad, :] = jnp.zeros((nb, pad, LRU_W), F32)
        h_scr[...] = jnp.zeros(h_scr.shape, F32)

    spl = _softplus(-lam_ref[...])
    row = lax.broadcasted_iota(jnp.int32, (tt, LRU_W), 0)
    for b in range(nb):
        xp_scr[b, pad:pad + tt, :] = xb_ref[b]
        xc = cb_ref[...] + xp_scr[b, pad - 3:pad - 3 + tt, :] * cw_ref[0:1, :]
        for j in range(1, CONV_W):
            xc = xc + xp_scr[b, pad - 3 + j:pad - 3 + j + tt, :] * cw_ref[j:j + 1, :]
        xp_scr[b, 0:pad, :] = xp_scr[b, tt:tt + pad, :]
        a, mult, gx = _lru_gates(xc, wa_ref, wx_ref, ba_ref[...], bx_ref[...], spl)
        mult = jnp.where((row == 0) & (i == 0), 1.0, mult)
        a_scr[b] = a
        b_scr[b] = mult * gx

    sub = lax.broadcasted_iota(jnp.int32, (8, LRU_W), 0)

    def body(s, hs):
        r0 = pl.multiple_of(s * 8, 8)
        a8 = [a_scr[b, pl.ds(r0, 8), :] for b in range(nb)]
        b8 = [b_scr[b, pl.ds(r0, 8), :] for b in range(nb)]
        hs = list(hs)
        for j in range(8):
            for b in range(nb):
                prev = pltpu.roll(hs[b], 1, 0)
                hs[b] = jnp.where(sub == j, a8[b] * prev + b8[b], hs[b])
        for b in range(nb):
            b_scr[b, pl.ds(r0, 8), :] = hs[b]
        return tuple(hs)

    hs = lax.fori_loop(0, tt // 8, body, tuple(h_scr[b] for b in range(nb)), unroll=2)
    for b in range(nb):
        h_scr[b] = hs[b]
        gb = gb_ref[b]
        y_ref[b] = (b_scr[b] * (gb * _sigmoid(gb))).astype(BF16)

    @pl.when(i == pl.num_programs(0) - 1)
    def _():
        for b in range(nb):
            hl_ref[b] = hs[b][7:8, :]


def _lru_prompt(p1, B, T, prm, *, tt):
    nt = T // tt
    vec = pl.BlockSpec((1, LRU_W), lambda i: (0, 0))
    wspec = pl.BlockSpec((LRU_BLOCKS, LRU_BW, LRU_BW), lambda i: (0, 0, 0))
    return pl.pallas_call(
        functools.partial(_lru_prompt_kernel, tt=tt, nb=B),
        grid=(nt,),
        in_specs=[
            pl.BlockSpec((B, tt, LRU_W), lambda i: (0, i, 0)),
            pl.BlockSpec((B, tt, LRU_W), lambda i: (0, i, 1)),
            pl.BlockSpec((CONV_W, LRU_W), lambda i: (0, 0)),
            vec, wspec, wspec, vec, vec, vec,
        ],
        out_specs=[pl.BlockSpec((B, tt, LRU_W), lambda i: (0, i, 0)),
                   pl.BlockSpec((B, 1, LRU_W), lambda i: (0, 0, 0))],
        out_shape=[jax.ShapeDtypeStruct((B, T, LRU_W), BF16), jax.ShapeDtypeStruct((B, 1, LRU_W), F32)],
        scratch_shapes=[pltpu.VMEM((B, tt + 8, LRU_W), F32), pltpu.VMEM((B, tt, LRU_W), F32),
                        pltpu.VMEM((B, tt, LRU_W), F32), pltpu.VMEM((B, 8, LRU_W), F32)],
        compiler_params=_cparams(("arbitrary",)),
        name="lru_prompt",
    )(p1, p1, prm["conv_w"], prm["conv_b"], prm["wa"], prm["wx"], prm["ba"], prm["bx"], prm["lam"])


def _lru_sample_kernel(xb_ref, gb_ref, cbuf_ref, h0_ref, cw_ref, cb_ref, wa_ref, wx_ref, ba_ref, bx_ref, lam_ref,
                       y_ref, hl_ref, *, T):
    spl = _softplus(-lam_ref[...])
    h = h0_ref[...]
    for t in range(T):
        xc = cb_ref[...]
        for j in range(CONV_W):
            src = t + j - (CONV_W - 1)
            tap = xb_ref[src] if src >= 0 else cbuf_ref[src + CONV_W - 1]
            xc = xc + tap * cw_ref[j:j + 1, :]
        a, mult, gx = _lru_gates(xc, wa_ref, wx_ref, ba_ref[...], bx_ref[...], spl)
        h = a * h + mult * gx
        gb = gb_ref[t]
        y_ref[t] = (h * (gb * _sigmoid(gb))).astype(BF16)
    hl_ref[...] = h


def _lru_sample(p1_tm, cbuf_tm, h0, prm, *, bt):
    T, B, _ = p1_tm.shape
    vec = pl.BlockSpec((1, LRU_W), lambda i: (0, 0))
    wspec = pl.BlockSpec((LRU_BLOCKS, LRU_BW, LRU_BW), lambda i: (0, 0, 0))
    return pl.pallas_call(
        functools.partial(_lru_sample_kernel, T=T),
        grid=(B // bt,),
        in_specs=[
            pl.BlockSpec((T, bt, LRU_W), lambda i: (0, i, 0)),
            pl.BlockSpec((T, bt, LRU_W), lambda i: (0, i, 1)),
            pl.BlockSpec((CONV_W - 1, bt, LRU_W), lambda i: (0, i, 0)),
            pl.BlockSpec((bt, LRU_W), lambda i: (i, 0)),
            pl.BlockSpec((CONV_W, LRU_W), lambda i: (0, 0)),
            vec, wspec, wspec, vec, vec, vec,
        ],
        out_specs=[pl.BlockSpec((T, bt, LRU_W), lambda i: (0, i, 0)), pl.BlockSpec((bt, LRU_W), lambda i: (i, 0))],
        out_shape=[jax.ShapeDtypeStruct((T, B, LRU_W), BF16), jax.ShapeDtypeStruct((B, LRU_W), F32)],
        compiler_params=_cparams(("parallel",)),
        name="lru_sample",
    )(p1_tm, p1_tm, cbuf_tm, h0, prm["conv_w"], prm["conv_b"], prm["wa"], prm["wx"], prm["ba"], prm["bx"], prm["lam"])


def kernel(x_prompt, x_sample, state_ret, state_rwkv_wkv, state_rwkv_shift, state_lru_h, state_lru_conv,
           norm0_g, w_in0, w_out0, rwkv_mu_rkv, rwkv_mu_w, rwkv_mu_a, rwkv_w0, rwkv_w1, rwkv_w2,
           rwkv_a0, rwkv_a1, rwkv_a2, rwkv_k_k, rwkv_k_a, rwkv_r_k, rwkv_gn_g, rwkv_gn_b,
           norm1_g, w_in1, lru_conv_w, lru_conv_b, lru_wa, lru_ba, lru_wx, lru_bx, lru_lambda, w_out1,
           final_g):
    bp, tp, d = x_prompt.shape
    bs, ts, _ = x_sample.shape
    xp = x_prompt.reshape(bp * tp, d)
    xs = x_sample.reshape(bs * ts, d)

    mw = rwkv_mu_w[:, None]
    ma = rwkv_mu_a[:, None]
    w_lora = jnp.concatenate([(1.0 - mw) * rwkv_w1, mw * rwkv_w1, (1.0 - ma) * rwkv_a1, ma * rwkv_a1],
                             axis=1).astype(BF16)
    w_in0_b = w_in0.astype(BF16)
    w_out0_b = w_out0.astype(BF16)
    w_in1_b = w_in1.astype(BF16)
    w_out1_b = w_out1.astype(BF16)
    row = lambda v: v.reshape(1, -1)
    rwkv_prm = dict(mu_rkv=rwkv_mu_rkv, w0=row(rwkv_w0), a0=row(rwkv_a0), k_k=row(rwkv_k_k), k_a=row(rwkv_k_a),
                    r_k=row(rwkv_r_k), gn_g=row(rwkv_gn_g), gn_b=row(rwkv_gn_b),
                    w2=rwkv_w2.astype(BF16), a2=rwkv_a2.astype(BF16))
    lru_prm = dict(conv_w=lru_conv_w, conv_b=row(lru_conv_b), wa=lru_wa.astype(BF16), wx=lru_wx.astype(BF16),
                   ba=row(lru_ba), bx=row(lru_bx), lam=row(lru_lambda))

    p0p, plp = _norm_matmul(xp, norm0_g, w_in0_b, w_lora, apply_norm=True, tm=_tile(bp * tp, 1024), tn=1024)
    p0s, pls = _norm_matmul(xs, norm0_g, w_in0_b, w_lora, apply_norm=True, tm=bs * ts, tn=1024)
    pfirst, pfirst_lora = _norm_matmul(state_rwkv_shift, norm0_g, w_in0_b, w_lora, apply_norm=False, tm=bs, tn=512)
    p0p = p0p.reshape(bp, tp, IN0)
    p0s = p0s.reshape(bs, ts, IN0)

    cp = RET_CHUNK if tp % RET_CHUNK == 0 else tp
    cs = RET_CHUNK if ts % RET_CHUNK == 0 else ts
    nbp = 2 if bp % 2 == 0 else 1
    ret_p, new_ret_p = _retention(p0p, bp, tp, cp, nbp, nbp, _ret_tables(np.arange(tp), cp), None)
    ret_s, new_ret_s = _retention(p0s, bs, ts, cs, 8, 4, _ret_tables(PAST_LEN + np.arange(ts), cs), state_ret)

    cwp = RWKV_CHUNK if tp % RWKV_CHUNK == 0 else tp
    rw_p, wkv_p = _rwkv(p0p, plp.reshape(bp, tp, 4 * LORA), bp, tp, cwp, bp, 4, None, None, None, rwkv_prm)
    rw_s, wkv_s = _rwkv(p0s, pls.reshape(bs, ts, 4 * LORA), bs, ts, ts, 8, 2, pfirst, pfirst_lora, state_rwkv_wkv,
                        rwkv_prm)
    ret_p = ret_p.reshape(bp * tp, RET_V)
    ret_s = ret_s.reshape(bs * ts, RET_V)
    rw_p = rw_p.reshape(bp * tp, RWKV_W)
    rw_s = rw_s.reshape(bs * ts, RWKV_W)

    x1p = _outproj0(xp, ret_p, rw_p, w_out0_b, tm=_tile(bp * tp, 1024), tn=1024)
    x1s = _outproj0(xs, ret_s, rw_s, w_out0_b, tm=512, tn=1024)

    last_rows = jnp.concatenate([x_prompt[:, -1, :], x_sample[:, -1, :]], axis=0)
    shift = _rmsnorm_rows(last_rows, norm0_g)

    p1p = _norm_matmul(x1p, norm1_g, w_in1_b, apply_norm=True, tm=_tile(bp * tp, 1024), tn=1024)
    x1s_tm = x1s.reshape(bs, ts, d).transpose(1, 0, 2).reshape(ts * bs, d)
    p1s = _norm_matmul(x1s_tm, norm1_g, w_in1_b, apply_norm=True, tm=512, tn=512)

    yp, lh_p = _lru_prompt(p1p.reshape(bp, tp, 2 * LRU_W), bp, tp, lru_prm, tt=128)
    yp = yp.reshape(bp * tp, LRU_W)
    ys_tm, lh_s = _lru_sample(p1s.reshape(ts, bs, 2 * LRU_W), state_lru_conv.transpose(1, 0, 2), state_lru_h,
                              lru_prm, bt=64)

    out_p = _outproj1_norm(x1p, yp, w_out1_b, final_g, tm=512)
    out_s_tm = _outproj1_norm(x1s_tm, ys_tm.reshape(ts * bs, LRU_W), w_out1_b, final_g, tm=256)

    y_prompt = out_p.reshape(bp, tp, d)
    y_sample = out_s_tm.reshape(ts, bs, d).transpose(1, 0, 2)
    lc_p = p1p.reshape(bp, tp, 2 * LRU_W)[:, tp - (CONV_W - 1):, :LRU_W]
    lc_s = p1s.reshape(ts, bs, 2 * LRU_W)[ts - (CONV_W - 1):, :, :LRU_W].transpose(1, 0, 2)
    return (y_prompt, y_sample,
            new_ret_p, wkv_p, shift[:bp], lh_p.reshape(bp, LRU_W), lc_p,
            new_ret_s, wkv_s, shift[bp:], lh_s, lc_s)
```
